```python
import math
import jax, jax.numpy as jnp
from jax import lax
import numpy as np

D_MODEL = 1024
BATCH = 8
SEQ = 8192
DEPTH = 1
DEC_BATCH = 32
DEC_SEQ = 16
PAST_LEN = 2048

CHUNK = 64
N_META = 16
Q_BLOCK = 128
MIX_WIDTH = D_MODEL
N_HEADS_A = 4
VAL_W_A = MIX_WIDTH // 2
HEAD_V_A = VAL_W_A // N_HEADS_A
HEAD_DIM_A = HEAD_V_A // 2
QK_W_A = N_HEADS_A * 2 * HEAD_DIM_A
ROT_DIM = HEAD_DIM_A // 4
ROPE_THETA = 500000.0
N_HEADS_B = 4
VAL_W_B = MIX_WIDTH - VAL_W_A
VAL_DIM_B = VAL_W_B // N_HEADS_B
KEY_DIM_B = VAL_DIM_B // 2
KEY_W_B = N_HEADS_B * KEY_DIM_B
GATE_RANK = 16
GATE_TAU = 16.0
D_FF = -(-8 * D_MODEL // (3 * 256)) * 256
EPS = 1e-6
SPLIT_SIZES = (QK_W_A, QK_W_A, VAL_W_A, KEY_W_B, KEY_W_B, VAL_W_B, VAL_W_B, GATE_RANK)
SPLIT_IDX = tuple(int(i) for i in np.cumsum(SPLIT_SIZES)[:-1])
N_IN = sum(SPLIT_SIZES)

kernel_name = 'hymba_diffattn_gla_streaming_step'


def rmsnorm(x, g):
    xf = x.astype(jnp.float32)
    y = xf * lax.rsqrt(jnp.mean(xf * xf, axis=-1, keepdims=True) + EPS)
    return (y * g.astype(jnp.float32)).astype(x.dtype)


def rope(x, pos):
    half = ROT_DIM // 2
    inv_freq = ROPE_THETA ** (-jnp.arange(0, ROT_DIM, 2, dtype=jnp.float32) / ROT_DIM)
    ang = pos.astype(jnp.float32)[:, None] * inv_freq[None, :]
    cos = jnp.cos(ang)[:, None, None, :].astype(x.dtype)
    sin = jnp.sin(ang)[:, None, None, :].astype(x.dtype)
    x1, x2, rest = x[..., :half], x[..., half:ROT_DIM], x[..., ROT_DIM:]
    return jnp.concatenate([x1 * cos - x2 * sin, x2 * cos + x1 * sin, rest], axis=-1)


def lambda_init(layer):
    return 0.8 - 0.6 * math.exp(-0.3 * layer)


def diff_lambda(lw, li):
    f32 = jnp.float32
    return (jnp.exp(jnp.sum(lw['lambda_q1'].astype(f32) * lw['lambda_k1'].astype(f32)))
            - jnp.exp(jnp.sum(lw['lambda_q2'].astype(f32) * lw['lambda_k2'].astype(f32))) + li)


def project(hn, pos, lw):
    lead = hn.shape[:-1]
    p = hn @ lw['w_in']
    qa, ka, va, qb, kb, vb, gb, a_low = jnp.split(p, SPLIT_IDX, axis=-1)
    qa = rope(rmsnorm(qa.reshape(lead + (N_HEADS_A, 2, HEAD_DIM_A)), lw['q_norm']), pos)
    ka = rope(rmsnorm(ka.reshape(lead + (N_HEADS_A, 2, HEAD_DIM_A)), lw['k_norm']), pos)
    va = va.reshape(lead + (N_HEADS_A, HEAD_V_A))
    qb = qb.reshape(lead + (N_HEADS_B, KEY_DIM_B)) * (KEY_DIM_B ** -0.5)
    kb = kb.reshape(lead + (N_HEADS_B, KEY_DIM_B))
    vb = vb.reshape(lead + (N_HEADS_B, VAL_DIM_B))
    log_a = (jax.nn.log_sigmoid((a_low @ lw['w_a2'] + lw['b_a']).astype(jnp.float32))
             / GATE_TAU).reshape(lead + (N_HEADS_B, KEY_DIM_B))
    return qa, ka, va, qb, kb, vb, gb, log_a


def diff_attend(q, k, v, lam, mask):
    s = jnp.einsum('...qhcd,...khcd->...hcqk', q, k).astype(jnp.float32) * (HEAD_DIM_A ** -0.5)
    if mask is not None:
        s = jnp.where(mask, s, -jnp.inf)
    p = jax.nn.softmax(s, axis=-1)
    a = p[..., 0, :, :] - lam * p[..., 1, :, :]
    return jnp.einsum('...hqk,...khe->...qhe', a.astype(v.dtype), v)


def gla_block(q, k, v, log_a, s):
    f32 = jnp.float32
    t = q.shape[-3]
    b = jnp.cumsum(log_a.astype(f32), axis=-3)
    b_last = b[..., -1, :, :]
    qf = q.astype(f32) * jnp.exp(b)
    kf = k.astype(f32)
    vf = v.astype(f32)
    sf = s.astype(f32)
    causal = jnp.tril(jnp.ones((t, t), dtype=bool))
    scores = jnp.where(causal, jnp.einsum('...thd,...shd->...hts', qf, kf * jnp.exp(-b)), 0.0)
    o = (jnp.einsum('...thd,...hde->...the', qf, sf)
         + jnp.einsum('...hts,...she->...the', scores, vf))
    s_new = (jnp.exp(b_last)[..., None] * sf
             + jnp.einsum('...thd,...the->...hde', kf * jnp.exp(b_last[..., None, :, :] - b), vf))
    return o.astype(v.dtype), s_new.astype(s.dtype)


def mix_out(o_a, o_b, gb, lw, li):
    lead = o_a.shape[:-2]
    ya = rmsnorm(o_a, lw['g_diff'].reshape(N_HEADS_A, HEAD_V_A)) * (1.0 - li)
    yb = rmsnorm(o_b, lw['g_gla'].reshape(N_HEADS_B, VAL_DIM_B)).reshape(lead + (VAL_W_B,))
    yb = yb * jax.nn.silu(gb)
    y = jnp.concatenate([ya.reshape(lead + (VAL_W_A,)), yb], axis=-1)
    return y @ lw['w_out']


def ffn(h, lw):
    hn = rmsnorm(h, lw['norm_ffn'])
    return (jax.nn.silu(hn @ lw['w_ffn_gate']) * (hn @ lw['w_ffn_up'])) @ lw['w_ffn_down']


def prompt_layer(hm, hx, lw, li):
    b_sz, s_len = hx.shape[0], hx.shape[1]
    lam = diff_lambda(lw, li)
    pos_m = jnp.arange(N_META, dtype=jnp.int32)
    pos_x = N_META + jnp.arange(s_len, dtype=jnp.int32)
    qa_m, ka_m, va_m, qb_m, kb_m, vb_m, gb_m, la_m = project(rmsnorm(hm, lw['norm_mix']), pos_m, lw)
    qa_x, ka_x, va_x, qb_x, kb_x, vb_x, gb_x, la_x = project(rmsnorm(hx, lw['norm_mix']), pos_x, lw)

    k_all = jnp.concatenate([jnp.broadcast_to(ka_m[None], (b_sz,) + ka_m.shape), ka_x], axis=1)
    v_all = jnp.concatenate([jnp.broadcast_to(va_m[None], (b_sz,) + va_m.shape), va_x], axis=1)
    oa_m = diff_attend(qa_m, ka_m, va_m, lam, None)
    key_chunk = jnp.concatenate([jnp.full((N_META,), -1, jnp.int32),
                                 jnp.arange(s_len, dtype=jnp.int32) // CHUNK])
    n_qb = s_len // Q_BLOCK
    q_blocks = qa_x.reshape((b_sz, n_qb, Q_BLOCK) + qa_x.shape[2:]).swapaxes(0, 1)
    q_chunk = (jnp.arange(s_len, dtype=jnp.int32) // CHUNK).reshape(n_qb, Q_BLOCK)

    def attend_block(args):
        q_blk, qc = args
        return diff_attend(q_blk, k_all, v_all, lam, qc[:, None] >= key_chunk[None, :])

    oa_x = lax.map(attend_block, (q_blocks, q_chunk)).swapaxes(0, 1).reshape(
        b_sz, s_len, N_HEADS_A, HEAD_V_A)

    s_zero = jnp.zeros((N_HEADS_B, KEY_DIM_B, VAL_DIM_B), vb_m.dtype)
    ob_m, s_meta = gla_block(qb_m, kb_m, vb_m, la_m, s_zero)

    def to_chunks(t):
        return t.reshape((b_sz, s_len // CHUNK, CHUNK) + t.shape[2:]).swapaxes(0, 1)

    def scan_step(s, blk):
        o, s_next = gla_block(blk[0], blk[1], blk[2], blk[3], s)
        return s_next, o

    s_final, ob_x = lax.scan(scan_step, jnp.broadcast_to(s_meta[None], (b_sz,) + s_meta.shape),
                             (to_chunks(qb_x), to_chunks(kb_x), to_chunks(vb_x), to_chunks(la_x)))
    ob_x = ob_x.swapaxes(0, 1).reshape(b_sz, s_len, N_HEADS_B, VAL_DIM_B)

    hm = hm + mix_out(oa_m, ob_m, gb_m, lw, li)
    hm = hm + ffn(hm, lw)
    hx = hx + mix_out(oa_x, ob_x, gb_x, lw, li)
    hx = hx + ffn(hx, lw)
    new_k = k_all.reshape(b_sz, N_META + s_len, N_HEADS_A, 2 * HEAD_DIM_A)
    return hm, hx, new_k, v_all, s_final


def sample_layer(hs, cache_k, cache_v, state, lw, li):
    db, t_new = hs.shape[0], hs.shape[1]
    past = cache_k.shape[1]
    lam = diff_lambda(lw, li)
    pos = past + jnp.arange(t_new, dtype=jnp.int32)
    qa, ka, va, qb, kb, vb, gb, la = project(rmsnorm(hs, lw['norm_mix']), pos, lw)
    k_all = jnp.concatenate([cache_k.reshape(db, past, N_HEADS_A, 2, HEAD_DIM_A), ka], axis=1)
    v_all = jnp.concatenate([cache_v, va], axis=1)
    oa = diff_attend(qa, k_all, v_all, lam, None)
    ob, s_new = gla_block(qb, kb, vb, la, state)
    hs = hs + mix_out(oa, ob, gb, lw, li)
    hs = hs + ffn(hs, lw)
    return hs, ka.reshape(db, t_new, N_HEADS_A, 2 * HEAD_DIM_A), va, s_new


def setup_inputs(seed: int = 0) -> dict:
    key = jax.random.key(seed)
    ks = jax.random.split(key, 24)
    f32 = jnp.float32

    def nrm(k, shape, scale):
        return jax.random.normal(k, shape, f32) * scale

    return {
        'x_prompt': nrm(ks[0], (BATCH, SEQ, D_MODEL), 1.0),
        'x_sample': nrm(ks[1], (DEC_BATCH, DEC_SEQ, D_MODEL), 1.0),
        'cache_k_diff': nrm(ks[2], (DEPTH, DEC_BATCH, N_META + PAST_LEN, N_HEADS_A, 2 * HEAD_DIM_A), 1.0),
        'cache_v_diff': nrm(ks[3], (DEPTH, DEC_BATCH, N_META + PAST_LEN, N_HEADS_A, HEAD_V_A), 1.0),
        'state_gla': nrm(ks[4], (DEPTH, DEC_BATCH, N_HEADS_B, KEY_DIM_B, VAL_DIM_B), 1.0),
        'meta_tokens': nrm(ks[5], (N_META, D_MODEL), 1.0),
        'norm_mix': 1.0 + nrm(ks[6], (DEPTH, D_MODEL), 0.02),
        'w_in': nrm(ks[7], (DEPTH, D_MODEL, N_IN), D_MODEL ** -0.5),
        'w_a2': nrm(ks[8], (DEPTH, GATE_RANK, KEY_W_B), GATE_RANK ** -0.5),
        'b_a': nrm(ks[9], (DEPTH, KEY_W_B), 0.01),
        'q_norm': 1.0 + nrm(ks[10], (DEPTH, HEAD_DIM_A), 0.02),
        'k_norm': 1.0 + nrm(ks[11], (DEPTH, HEAD_DIM_A), 0.02),
        'lambda_q1': nrm(ks[12], (DEPTH, HEAD_DIM_A), 0.1),
        'lambda_k1': nrm(ks[13], (DEPTH, HEAD_DIM_A), 0.1),
        'lambda_q2': nrm(ks[14], (DEPTH, HEAD_DIM_A), 0.1),
        'lambda_k2': nrm(ks[15], (DEPTH, HEAD_DIM_A), 0.1),
        'g_diff': 1.0 + nrm(ks[16], (DEPTH, VAL_W_A), 0.02),
        'g_gla': 1.0 + nrm(ks[17], (DEPTH, VAL_W_B), 0.02),
        'w_out': nrm(ks[18], (DEPTH, MIX_WIDTH, D_MODEL), MIX_WIDTH ** -0.5),
        'norm_ffn': 1.0 + nrm(ks[19], (DEPTH, D_MODEL), 0.02),
        'w_ffn_gate': nrm(ks[20], (DEPTH, D_MODEL, D_FF), D_MODEL ** -0.5),
        'w_ffn_up': nrm(ks[21], (DEPTH, D_MODEL, D_FF), D_MODEL ** -0.5),
        'w_ffn_down': nrm(ks[22], (DEPTH, D_FF, D_MODEL), D_FF ** -0.5),
    }


def reference(x_prompt, x_sample, cache_k_diff, cache_v_diff, state_gla, meta_tokens, norm_mix,
              w_in, w_a2, b_a, q_norm, k_norm, lambda_q1, lambda_k1, lambda_q2, lambda_k2,
              g_diff, g_gla, w_out, norm_ffn, w_ffn_gate, w_ffn_up, w_ffn_down):
    hm, hx, hs = meta_tokens, x_prompt, x_sample
    k_p, v_p, s_p, k_s, v_s, s_s = [], [], [], [], [], []
    for layer in range(DEPTH):
        lw = {
            'norm_mix': norm_mix[layer], 'w_in': w_in[layer], 'w_a2': w_a2[layer], 'b_a': b_a[layer],
            'q_norm': q_norm[layer], 'k_norm': k_norm[layer],
            'lambda_q1': lambda_q1[layer], 'lambda_k1': lambda_k1[layer],
            'lambda_q2': lambda_q2[layer], 'lambda_k2': lambda_k2[layer],
            'g_diff': g_diff[layer], 'g_gla': g_gla[layer], 'w_out': w_out[layer],
            'norm_ffn': norm_ffn[layer], 'w_ffn_gate': w_ffn_gate[layer],
            'w_ffn_up': w_ffn_up[layer], 'w_ffn_down': w_ffn_down[layer],
        }
        li = lambda_init(layer)
        hm, hx, kp, vp, sp = prompt_layer(hm, hx, lw, li)
        hs, kn, vn, sn = sample_layer(hs, cache_k_diff[layer], cache_v_diff[layer], state_gla[layer], lw, li)
        k_p.append(kp); v_p.append(vp); s_p.append(sp)
        k_s.append(kn); v_s.append(vn); s_s.append(sn)
    return (hx, hs, jnp.stack(k_p), jnp.stack(v_p), jnp.stack(s_p),
            jnp.stack(k_s), jnp.stack(v_s), jnp.stack(s_s))
```

```python
import functools
import math

import jax
import jax.numpy as jnp
from jax import lax
from jax.experimental import pallas as pl
from jax.experimental.pallas import tpu as pltpu

F32 = jnp.float32
BF16 = jnp.bfloat16

D_MODEL = 1024
N_HEADS_A = 4
HEAD_DIM_A = 64
HEAD_V_A = 128
QK_W_A = 512
VAL_W_A = 512
ROT_DIM = 16
ROPE_THETA = 500000.0
N_HEADS_B = 4
KEY_DIM_B = 64
VAL_DIM_B = 128
KEY_W_B = 256
VAL_W_B = 512
GATE_RANK = 16
GATE_TAU = 16.0
D_FF = 2816
EPS = 1e-6
CHUNK = 64
LAMBDA_INIT = 0.8 - 0.6 * math.exp(-0.3 * 0)
N_MAIN = 2 * QK_W_A + VAL_W_A + 2 * KEY_W_B + 2 * VAL_W_B
LANES = 128
NEG = -1e30

VMEM_LIMIT = 56 * 1024 * 1024


def _cparams(n_axes):
    return pltpu.CompilerParams(dimension_semantics=("arbitrary",) * n_axes,
                                vmem_limit_bytes=VMEM_LIMIT)


def _dot(a, b):
    return jnp.dot(a, b, preferred_element_type=F32)


def _dot_nt(a, b):
    return lax.dot_general(a, b, (((1,), (1,)), ((), ())), preferred_element_type=F32)


def _dot_tn(a, b):
    return lax.dot_general(a, b, (((0,), (0,)), ((), ())), preferred_element_type=F32)


def _split_bf16(x):
    hi = x.astype(BF16)
    lo = (x - hi.astype(F32)).astype(BF16)
    return hi, lo


def _diff_lambda(lam_ref):
    l4 = lam_ref[...]
    s1 = jnp.sum(l4[0:1] * l4[1:2], axis=-1, keepdims=True)
    s2 = jnp.sum(l4[2:3] * l4[3:4], axis=-1, keepdims=True)
    return jnp.exp(s1) - jnp.exp(s2) + LAMBDA_INIT


def _proj_kernel(x_ref, rope_ref, gmix_ref, wmain_ref, wlow_ref, wa2_ref, ba_ref, gq_ref, gk_ref,
                 gsum_ref, *out_refs, transposed, tq, tk):
    x = x_ref[0]
    tm = x.shape[0]
    ms = jnp.mean(x * x, axis=-1, keepdims=True)
    hn = (x * lax.rsqrt(ms + EPS) * gmix_ref[...]).astype(BF16)

    def seg(lo, hi):
        return _dot(hn, wmain_ref[:, lo:hi])

    cos = rope_ref[:, 0:LANES]
    sin_a = rope_ref[:, LANES:2 * LANES]
    sin_b = rope_ref[:, 2 * LANES:3 * LANES]
    gsum = gsum_ref[...]

    def qknorm_rope(p, g_ref):
        hi, lo = _split_bf16(p * p)
        ss = _dot(hi, gsum) + _dot(lo, gsum)
        pn = p * lax.rsqrt(ss * (1.0 / HEAD_DIM_A) + EPS) * g_ref[...]
        cols = []
        for c in range(QK_W_A // LANES):
            blk = pn[:, c * LANES:(c + 1) * LANES]
            cols.append(blk * cos + pltpu.roll(blk, LANES - ROT_DIM // 2, 1) * sin_a
                        + pltpu.roll(blk, ROT_DIM // 2, 1) * sin_b)
        return jnp.concatenate(cols, axis=1)

    o = 0
    qa = qknorm_rope(seg(o, o + QK_W_A), gq_ref); o += QK_W_A
    ka = qknorm_rope(seg(o, o + QK_W_A), gk_ref); o += QK_W_A
    va = seg(o, o + VAL_W_A); o += VAL_W_A
    qb = seg(o, o + KEY_W_B); o += KEY_W_B
    kb = seg(o, o + KEY_W_B); o += KEY_W_B
    vb = seg(o, o + VAL_W_B); o += VAL_W_B
    gb = seg(o, o + VAL_W_B); o += VAL_W_B
    a_low = _dot(hn, wlow_ref[...])
    gate = _dot(a_low.astype(BF16), wa2_ref[...]) + ba_ref[...]
    log_a = (jnp.minimum(gate, 0.0) - jnp.log(1.0 + jnp.exp(-jnp.abs(gate)))) * (1.0 / GATE_TAU)
    sg = gb * (1.0 / (1.0 + jnp.exp(-gb)))
    qa = qa * (HEAD_DIM_A ** -0.5)
    qb = qb * (KEY_DIM_B ** -0.5)

    if transposed:
        qT_ref, k_ref, vT_ref, kf_ref, vf_ref, qb_ref, kb_ref, vb_ref, sg_ref, la_ref = out_refs
        for r in range(tm // tq):
            qT_ref[0, r] = qa[r * tq:(r + 1) * tq, :].T.astype(BF16)
        for r in range(tm // tk):
            vT_ref[0, r] = va[r * tk:(r + 1) * tk, :].T.astype(BF16)
        k_ref[0] = ka.astype(BF16)
    else:
        q_ref, kf_ref, vf_ref, qb_ref, kb_ref, vb_ref, sg_ref, la_ref = out_refs
        q_ref[0] = qa.astype(BF16)
    kf_ref[0] = ka
    vf_ref[0] = va
    qb_ref[0] = qb.astype(BF16)
    kb_ref[0] = kb.astype(BF16)
    vb_ref[0] = vb.astype(BF16)
    sg_ref[0] = sg.astype(BF16)
    la_ref[0] = log_a


def _proj(x, rope, w, *, tm, transposed, tq=256, tk=256):
    nb, s, d = x.shape
    assert s % tm == 0
    grid = (s // tm, nb)

    def tok(width, dtype):
        return (jax.ShapeDtypeStruct((nb, s, width), dtype),
                pl.BlockSpec((1, tm, width), lambda i, b: (b, i, 0)))

    def const(a):
        return pl.BlockSpec(a.shape, lambda i, b: (0,) * a.ndim)

    outs = []
    if transposed:
        assert tm % tq == 0 and tm % tk == 0
        outs.append((jax.ShapeDtypeStruct((nb, s // tq, QK_W_A, tq), BF16),
                     pl.BlockSpec((1, tm // tq, QK_W_A, tq), lambda i, b: (b, i, 0, 0))))
        outs.append(tok(QK_W_A, BF16))
        outs.append((jax.ShapeDtypeStruct((nb, s // tk, VAL_W_A, tk), BF16),
                     pl.BlockSpec((1, tm // tk, VAL_W_A, tk), lambda i, b: (b, i, 0, 0))))
    else:
        outs.append(tok(QK_W_A, BF16))
    outs += [tok(QK_W_A, F32), tok(VAL_W_A, F32), tok(KEY_W_B, BF16), tok(KEY_W_B, BF16),
             tok(VAL_W_B, BF16), tok(VAL_W_B, BF16), tok(KEY_W_B, F32)]
    consts = [w['gmix'], w['w_main'], w['w_low'], w['w_a2'], w['b_a'], w['gq'], w['gk'], w['gsum']]
    return pl.pallas_call(
        functools.partial(_proj_kernel, transposed=transposed, tq=tq, tk=tk),
        grid=grid,
        in_specs=[pl.BlockSpec((1, tm, d), lambda i, b: (b, i, 0)),
                  pl.BlockSpec((tm, 3 * LANES), lambda i, b: (i, 0))] + [const(a) for a in consts],
        out_specs=[o[1] for o in outs],
        out_shape=[o[0] for o in outs],
        compiler_params=_cparams(2),
        name="proj_T" if transposed else "proj_small",
    )(x, rope, *consts)


def _attn_kernel(lam_ref, qT_ref, k_ref, vT_ref, km_ref, vmT_ref, g_ref, o_ref,
                 m_sc, l_sc, acc_sc, *, tq, tk, n_meta):
    i = pl.program_id(2)
    qT = qT_ref[0, 0]
    row = lax.broadcasted_iota(jnp.int32, qT.shape, 0)
    zero = jnp.zeros_like(qT)
    qz = jnp.concatenate([jnp.where(row < HEAD_DIM_A, qT, zero),
                          jnp.where(row >= HEAD_DIM_A, qT, zero)], axis=1)

    s = _dot(km_ref[...], qz)
    krow = lax.broadcasted_iota(jnp.int32, s.shape, 0)
    s = jnp.where(krow < n_meta, s, NEG)
    m0 = jnp.max(s, axis=0, keepdims=True)
    p = jnp.exp(s - m0)
    m_sc[...] = m0
    l_sc[...] = jnp.sum(p, axis=0, keepdims=True)
    acc_sc[...] = _dot(vmT_ref[0], p.astype(BF16))

    def step(j, mask):
        kt = k_ref[0, pl.ds(pl.multiple_of(j * tk, tk), tk), :]
        s = _dot(kt, qz)
        if mask is not None:
            s = jnp.where(mask, s, NEG)
        m_old = m_sc[...]
        m_new = jnp.maximum(m_old, jnp.max(s, axis=0, keepdims=True))
        alpha = jnp.exp(m_old - m_new)
        p = jnp.exp(s - m_new)
        m_sc[...] = m_new
        l_sc[...] = alpha * l_sc[...] + jnp.sum(p, axis=0, keepdims=True)
        acc_sc[...] = alpha * acc_sc[...] + _dot(vT_ref[0, j], p.astype(BF16))

    r = tq // tk
    n_full = i * r

    def body(j, carry):
        step(j, None)
        return carry

    lax.fori_loop(0, n_full, body, 0)
    kr = lax.broadcasted_iota(jnp.int32, (tk, 2 * tq), 0)
    qc = lax.broadcasted_iota(jnp.int32, (tk, 2 * tq), 1)
    qc = jnp.where(qc >= tq, qc - tq, qc) // CHUNK
    for d in range(r):
        step(n_full + d, (kr + d * tk) // CHUNK <= qc)

    lam = _diff_lambda(lam_ref)
    o = acc_sc[...] * (1.0 / l_sc[...])
    oT = o[:, :tq] - lam * o[:, tq:]
    ms = jnp.mean(oT * oT, axis=0, keepdims=True)
    on = (oT * lax.rsqrt(ms + EPS)).T
    o_ref[0] = (on * g_ref[...] * (1.0 - LAMBDA_INIT)).astype(BF16)


def _attn(lam4, qT, k, vT, km, vmT, g_diff, *, n_meta):
    nb, nq, _, tq = qT.shape
    _, nk, _, tk = vT.shape
    s = k.shape[1]
    grid = (nb, N_HEADS_A, nq)
    return pl.pallas_call(
        functools.partial(_attn_kernel, tq=tq, tk=tk, n_meta=n_meta),
        grid=grid,
        in_specs=[pl.BlockSpec(lam4.shape, lambda b, h, i: (0, 0)),
                  pl.BlockSpec((1, 1, HEAD_V_A, tq), lambda b, h, i: (b, i, h, 0)),
                  pl.BlockSpec((1, s, HEAD_V_A), lambda b, h, i: (b, 0, h)),
                  pl.BlockSpec((1, nk, HEAD_V_A, tk), lambda b, h, i: (b, 0, h, 0)),
                  pl.BlockSpec((LANES, HEAD_V_A), lambda b, h, i: (0, h)),
                  pl.BlockSpec((1, HEAD_V_A, LANES), lambda b, h, i: (h, 0, 0)),
                  pl.BlockSpec((1, HEAD_V_A), lambda b, h, i: (0, h))],
        out_specs=pl.BlockSpec((1, tq, HEAD_V_A), lambda b, h, i: (b, i, h)),
        out_shape=jax.ShapeDtypeStruct((nb, s, VAL_W_A), BF16),
        scratch_shapes=[pltpu.VMEM((1, 2 * tq), F32), pltpu.VMEM((1, 2 * tq), F32),
                        pltpu.VMEM((HEAD_V_A, 2 * tq), F32)],
        compiler_params=_cparams(3),
        name="attn",
    )(lam4, qT, k, vT, km, vmT, g_diff)


def _sattn_kernel(lam_ref, q_ref, ck_ref, cv_ref, kn_ref, vn_ref, g_ref, o_ref, *, p_main, p_all):
    lam = _diff_lambda(lam_ref)
    lane = lax.broadcasted_iota(jnp.int32, (1, LANES), 1)
    for h in range(N_HEADS_A):
        hs = slice(h * HEAD_V_A, (h + 1) * HEAD_V_A)
        qh = q_ref[0, :, hs]
        ks = [ck_ref[0, 0:p_main, hs]]
        vs = [cv_ref[0, 0:p_main, hs]]
        if p_all > p_main:
            ks.append(jnp.concatenate([ck_ref[0, p_main:p_all, hs], kn_ref[0, :, hs]], axis=0))
            vs.append(jnp.concatenate([cv_ref[0, p_main:p_all, hs], vn_ref[0, :, hs]], axis=0))
        else:
            ks.append(kn_ref[0, :, hs])
            vs.append(vn_ref[0, :, hs])
        ks = [a.astype(BF16) for a in ks]
        vs = [a.astype(BF16) for a in vs]
        outs = []
        for c in range(2):
            sel = (lane >= c * HEAD_DIM_A) & (lane < (c + 1) * HEAD_DIM_A)
            qc = jnp.where(sel, qh, jnp.zeros_like(qh))
            ss = [_dot_nt(qc, kk) for kk in ks]
            m = functools.reduce(jnp.maximum, [jnp.max(a, axis=-1, keepdims=True) for a in ss])
            ps = [jnp.exp(a - m) for a in ss]
            l = sum(jnp.sum(a, axis=-1, keepdims=True) for a in ps)
            acc = sum(_dot(a.astype(BF16), vv) for a, vv in zip(ps, vs))
            outs.append(acc * (1.0 / l))
        o = outs[0] - lam * outs[1]
        ms = jnp.mean(o * o, axis=-1, keepdims=True)
        y = o * lax.rsqrt(ms + EPS) * g_ref[:, hs] * (1.0 - LAMBDA_INIT)
        o_ref[0, :, hs] = y.astype(BF16)


def _sattn(lam4, q, ck, cv, kn, vn, g_diff):
    db, t, _ = q.shape
    p_all = ck.shape[1]
    p_main = (p_all // LANES) * LANES
    assert (p_all - p_main + t) % 16 == 0

    def tokspec(n):
        return pl.BlockSpec((1, n, VAL_W_A), lambda b: (b, 0, 0))

    return pl.pallas_call(
        functools.partial(_sattn_kernel, p_main=p_main, p_all=p_all),
        grid=(db,),
        in_specs=[pl.BlockSpec(lam4.shape, lambda b: (0, 0)), tokspec(t), tokspec(p_all),
                  tokspec(p_all), tokspec(t), tokspec(t),
                  pl.BlockSpec((1, VAL_W_A), lambda b: (0, 0))],
        out_specs=tokspec(t),
        out_shape=jax.ShapeDtypeStruct((db, t, VAL_W_A), BF16),
        compiler_params=_cparams(1),
        name="sattn",
    )(lam4, q, ck, cv, kn, vn, g_diff)


def _gla_kernel(q_ref, k_ref, v_ref, la_ref, sg_ref, s0_ref, g_ref, lbd_ref, y_ref, sout_ref,
                st_sc, o_sc, *, n_chunks):
    t = pl.program_id(1)
    n_pairs = N_HEADS_B // 2

    @pl.when(t == 0)
    def _():
        for p in range(n_pairs):
            st_sc[p] = s0_ref[0, p].T

    la = la_ref[0]
    hi, lo = _split_bf16(la)
    lbd = lbd_ref[...]
    b = _dot(lbd, hi) + _dot(lbd, lo)
    q = q_ref[0].astype(F32)
    k = k_ref[0].astype(F32)
    qf = q * jnp.exp(b)
    kdec = (k * jnp.exp(-b)).astype(BF16)
    lane = lax.broadcasted_iota(jnp.int32, (1, LANES), 1)
    tr = lax.broadcasted_iota(jnp.int32, (CHUNK, CHUNK), 0)
    tc = lax.broadcasted_iota(jnp.int32, (CHUNK, CHUNK), 1)
    causal = tc <= tr
    for c in range(n_chunks):
        rows = slice(c * CHUNK, (c + 1) * CHUNK)
        b_last = b[(c + 1) * CHUNK - 1:(c + 1) * CHUNK, :]
        decay = jnp.exp(b_last)
        kd2 = k[rows] * jnp.exp(b_last - b[rows])
        for p in range(n_pairs):
            ls = slice(p * LANES, (p + 1) * LANES)
            st = st_sc[p]
            st_bf = st.astype(BF16)
            qf_p = qf[rows, ls]
            kdec_p = kdec[rows, ls]
            kd2_p = kd2[:, ls]
            ds = jnp.zeros_like(st)
            for a in range(2):
                h = 2 * p + a
                sel = (lane >= a * KEY_DIM_B) & (lane < (a + 1) * KEY_DIM_B)
                q_h = jnp.where(sel, qf_p, 0.0).astype(BF16)
                sc = _dot_nt(q_h, kdec_p)
                sc = jnp.where(causal, sc, 0.0).astype(BF16)
                v_h = v_ref[0, rows, h * VAL_DIM_B:(h + 1) * VAL_DIM_B]
                o_sc[rows, h * VAL_DIM_B:(h + 1) * VAL_DIM_B] = _dot_nt(q_h, st_bf) + _dot(sc, v_h)
                kd2_h = jnp.where(sel, kd2_p, 0.0).astype(BF16)
                ds = ds + _dot_tn(v_h, kd2_h)
            st_sc[p] = st * decay[:, ls] + ds

    for h in range(N_HEADS_B):
        hs = slice(h * VAL_DIM_B, (h + 1) * VAL_DIM_B)
        o = o_sc[:, hs]
        ms = jnp.mean(o * o, axis=-1, keepdims=True)
        y = o * lax.rsqrt(ms + EPS) * g_ref[:, hs]
        y_ref[0, :, hs] = (y * sg_ref[0, :, hs].astype(F32)).astype(BF16)

    @pl.when(t == pl.num_programs(1) - 1)
    def _():
        for p in range(n_pairs):
            sout_ref[0, p] = st_sc[p].T


def _gla(q, k, v, la, sg, s0, g_gla, *, tg):
    nb, s, _ = q.shape
    assert s % tg == 0 and tg % CHUNK == 0
    n_chunks = tg // CHUNK
    s0p = s0.reshape(s0.shape[0], N_HEADS_B // 2, 2 * KEY_DIM_B, VAL_DIM_B)
    shared = s0p.shape[0] == 1
    idx = jnp.arange(tg)
    lbd = ((idx[:, None] // CHUNK == idx[None, :] // CHUNK)
           & (idx[None, :] <= idx[:, None])).astype(BF16)

    def tok(width):
        return pl.BlockSpec((1, tg, width), lambda b, t: (b, t, 0))

    sspec_in = pl.BlockSpec((1,) + s0p.shape[1:], (lambda b, t: (0, 0, 0, 0)) if shared
                            else (lambda b, t: (b, 0, 0, 0)))
    y, s_out = pl.pallas_call(
        functools.partial(_gla_kernel, n_chunks=n_chunks),
        grid=(nb, s // tg),
        in_specs=[tok(KEY_W_B), tok(KEY_W_B), tok(VAL_W_B), tok(KEY_W_B), tok(VAL_W_B), sspec_in,
                  pl.BlockSpec((1, VAL_W_B), lambda b, t: (0, 0)),
                  pl.BlockSpec((tg, tg), lambda b, t: (0, 0))],
        out_specs=[tok(VAL_W_B), pl.BlockSpec((1,) + s0p.shape[1:], lambda b, t: (b, 0, 0, 0))],
        out_shape=[jax.ShapeDtypeStruct((nb, s, VAL_W_B), BF16),
                   jax.ShapeDtypeStruct((nb,) + s0p.shape[1:], F32)],
        scratch_shapes=[pltpu.VMEM((N_HEADS_B // 2, VAL_DIM_B, 2 * KEY_DIM_B), F32),
                        pltpu.VMEM((tg, VAL_W_B), F32)],
        compiler_params=_cparams(2),
        name="gla",
    )(q, k, v, la, sg, s0p, g_gla, lbd)
    return y, s_out.reshape(nb, N_HEADS_B, KEY_DIM_B, VAL_DIM_B)


def _ffn_kernel(x_ref, ya_ref, yb_ref, woa_ref, wob_ref, gn_ref, wg_ref, wu_ref, wd_ref, o_ref,
                *, n_split):
    h = x_ref[...] + _dot(ya_ref[...], woa_ref[...]) + _dot(yb_ref[...], wob_ref[...])
    ms = jnp.mean(h * h, axis=-1, keepdims=True)
    hn = (h * lax.rsqrt(ms + EPS) * gn_ref[...]).astype(BF16)
    w = D_FF // n_split
    f = None
    for c in range(n_split):
        cs = slice(c * w, (c + 1) * w)
        g = _dot(hn, wg_ref[:, cs])
        u = _dot(hn, wu_ref[:, cs])
        a = (g * (1.0 / (1.0 + jnp.exp(-g))) * u).astype(BF16)
        d = _dot(a, wd_ref[cs, :])
        f = d if f is None else f + d
    o_ref[...] = h + f


def _ffn(x, ya, yb, w, *, tm, n_split=2):
    t, d = x.shape
    assert t % tm == 0

    def const(a):
        return pl.BlockSpec(a.shape, lambda i: (0,) * a.ndim)

    consts = [w['w_out_a'], w['w_out_b'], w['g_ffn'], w['w_gate'], w['w_up'], w['w_down']]
    return pl.pallas_call(
        functools.partial(_ffn_kernel, n_split=n_split),
        grid=(t // tm,),
        in_specs=[pl.BlockSpec((tm, d), lambda i: (i, 0)),
                  pl.BlockSpec((tm, VAL_W_A), lambda i: (i, 0)),
                  pl.BlockSpec((tm, VAL_W_B), lambda i: (i, 0))] + [const(a) for a in consts],
        out_specs=pl.BlockSpec((tm, d), lambda i: (i, 0)),
        out_shape=jax.ShapeDtypeStruct((t, d), F32),
        compiler_params=_cparams(1),
        name="ffn",
    )(x, ya, yb, *consts)


def _rope_table(pos):
    half = ROT_DIM // 2
    inv_freq = ROPE_THETA ** (-jnp.arange(0, ROT_DIM, 2, dtype=F32) / ROT_DIM)
    ang = pos.astype(F32)[:, None] * inv_freq[None, :]
    cos, sin = jnp.cos(ang), jnp.sin(ang)
    lane = jnp.arange(LANES) % HEAD_DIM_A
    j = lane % half
    c = jnp.where(lane[None, :] < ROT_DIM, cos[:, j], 1.0)
    sa = jnp.where(lane[None, :] < half, -sin[:, j], 0.0)
    sb = jnp.where((lane[None, :] >= half) & (lane[None, :] < ROT_DIM), sin[:, j], 0.0)
    return jnp.concatenate([c, sa, sb], axis=1).astype(F32)


def kernel(x_prompt, x_sample, cache_k_diff, cache_v_diff, state_gla, meta_tokens, norm_mix, w_in,
           w_a2, b_a, q_norm, k_norm, lambda_q1, lambda_k1, lambda_q2, lambda_k2, g_diff, g_gla,
           w_out, norm_ffn, w_ffn_gate, w_ffn_up, w_ffn_down):
    depth = w_in.shape[0]
    assert depth == 1, "single-layer step"
    bsz, seq, d = x_prompt.shape
    db, dt, _ = x_sample.shape
    n_meta = meta_tokens.shape[0]
    past = cache_k_diff.shape[2]
    assert d == D_MODEL and n_meta <= LANES and seq % CHUNK == 0

    w_in0 = w_in[0]
    gid = jnp.arange(QK_W_A) // HEAD_DIM_A
    w = {
        'gmix': norm_mix[0][None, :],
        'w_main': w_in0[:, :N_MAIN].astype(BF16),
        'w_low': jnp.pad(w_in0[:, N_MAIN:], ((0, 0), (0, LANES - GATE_RANK))).astype(BF16),
        'w_a2': jnp.pad(w_a2[0], ((0, LANES - GATE_RANK), (0, 0))).astype(BF16),
        'b_a': b_a[0][None, :],
        'gq': jnp.tile(q_norm[0], QK_W_A // HEAD_DIM_A)[None, :],
        'gk': jnp.tile(k_norm[0], QK_W_A // HEAD_DIM_A)[None, :],
        'gsum': (gid[:, None] == gid[None, :]).astype(BF16),
        'w_out_a': w_out[0][:VAL_W_A].astype(BF16),
        'w_out_b': w_out[0][VAL_W_A:].astype(BF16),
        'g_ffn': norm_ffn[0][None, :],
        'w_gate': w_ffn_gate[0].astype(BF16),
        'w_up': w_ffn_up[0].astype(BF16),
        'w_down': w_ffn_down[0].astype(BF16),
    }
    lam4 = jnp.stack([lambda_q1[0], lambda_k1[0], lambda_q2[0], lambda_k2[0]])
    g_diff2 = g_diff[0][None, :]
    g_gla2 = g_gla[0][None, :]

    rope_m = _rope_table(jnp.arange(n_meta, dtype=jnp.int32))
    (_, kf_m, vf_m, qb_m, kb_m, vb_m, sg_m, la_m) = _proj(
        meta_tokens[None], rope_m, w, tm=n_meta, transposed=False)
    pad_m = ((0, 0), (0, CHUNK - n_meta), (0, 0))
    s_zero = jnp.zeros((1, N_HEADS_B, KEY_DIM_B, VAL_DIM_B), F32)
    _, s_meta = _gla(jnp.pad(qb_m, pad_m), jnp.pad(kb_m, pad_m), jnp.pad(vb_m, pad_m),
                     jnp.pad(la_m, pad_m), jnp.pad(sg_m, pad_m), s_zero, g_gla2, tg=CHUNK)

    tm = 512 if seq % 512 == 0 else seq
    tqk = 512 if seq % 512 == 0 else seq
    rope_x = _rope_table(n_meta + jnp.arange(seq, dtype=jnp.int32))
    (qT, k_bf, vT, kf_x, vf_x, qb_x, kb_x, vb_x, sg_x, la_x) = _proj(
        x_prompt, rope_x, w, tm=tm, transposed=True, tq=tqk, tk=tqk)
    km = jnp.pad(kf_m[0], ((0, LANES - n_meta), (0, 0))).astype(BF16)
    vmT = jnp.pad(vf_m[0].reshape(n_meta, N_HEADS_A, HEAD_V_A).transpose(1, 2, 0),
                  ((0, 0), (0, 0), (0, LANES - n_meta))).astype(BF16)
    ya_x = _attn(lam4, qT, k_bf, vT, km, vmT, g_diff2, n_meta=n_meta)
    yb_x, s_final = _gla(qb_x, kb_x, vb_x, la_x, sg_x, s_meta, g_gla2, tg=tm)
    y_prompt = _ffn(x_prompt.reshape(bsz * seq, d), ya_x.reshape(bsz * seq, VAL_W_A),
                    yb_x.reshape(bsz * seq, VAL_W_B), w, tm=tm).reshape(bsz, seq, d)
    new_k_p = jnp.concatenate([jnp.broadcast_to(kf_m, (bsz, n_meta, QK_W_A)), kf_x], axis=1)
    new_v_p = jnp.concatenate([jnp.broadcast_to(vf_m, (bsz, n_meta, VAL_W_A)), vf_x], axis=1)

    ts = db * dt
    rope_s = jnp.tile(_rope_table(past + jnp.arange(dt, dtype=jnp.int32)), (db, 1))
    (q_s, kf_s, vf_s, qb_s, kb_s, vb_s, sg_s, la_s) = _proj(
        x_sample.reshape(1, ts, d), rope_s, w, tm=ts, transposed=False)
    ya_s = _sattn(lam4, q_s.reshape(db, dt, QK_W_A), cache_k_diff[0].reshape(db, past, QK_W_A),
                  cache_v_diff[0].reshape(db, past, VAL_W_A), kf_s.reshape(db, dt, QK_W_A),
                  vf_s.reshape(db, dt, VAL_W_A), g_diff2)
    pad_s = ((0, 0), (0, CHUNK - dt), (0, 0))

    def stream(a):
        return jnp.pad(a.reshape(db, dt, a.shape[-1]), pad_s)

    yb_s, s_new = _gla(stream(qb_s), stream(kb_s), stream(vb_s), stream(la_s), stream(sg_s),
                       state_gla[0], g_gla2, tg=CHUNK)
    y_sample = _ffn(x_sample.reshape(ts, d), ya_s.reshape(ts, VAL_W_A),
                    yb_s[:, :dt].reshape(ts, VAL_W_B), w, tm=ts).reshape(db, dt, d)

    return (y_prompt, y_sample,
            new_k_p.reshape(1, bsz, n_meta + seq, N_HEADS_A, 2 * HEAD_DIM_A),
            new_v_p.reshape(1, bsz, n_meta + seq, N_HEADS_A, HEAD_V_A),
            s_final[None],
            kf_s.reshape(1, db, dt, N_HEADS_A, 2 * HEAD_DIM_A),
            vf_s.reshape(1, db, dt, N_HEADS_A, HEAD_V_A),
            s_new[None])
```

```python
import functools
import math

import jax
import jax.numpy as jnp
from jax import lax
from jax.experimental import pallas as pl
from jax.experimental.pallas import tpu as pltpu

F32 = jnp.float32
BF16 = jnp.bfloat16

D_MODEL = 1024
N_HEADS_A = 4
HEAD_DIM_A = 64
HEAD_V_A = 128
QK_W_A = 512
VAL_W_A = 512
ROT_DIM = 16
ROPE_THETA = 500000.0
N_HEADS_B = 4
KEY_DIM_B = 64
VAL_DIM_B = 128
KEY_W_B = 256
VAL_W_B = 512
GATE_RANK = 16
GATE_TAU = 16.0
D_FF = 2816
EPS = 1e-6
CHUNK = 64
LAMBDA_INIT = 0.8 - 0.6 * math.exp(-0.3 * 0)
N_MAIN = 2 * QK_W_A + VAL_W_A + 2 * KEY_W_B + 2 * VAL_W_B
LANES = 128
NEG = -1e30
LOG2E = 1.4426950408889634
ATTN_COL_BLOCK = 256

VMEM_LIMIT = 56 * 1024 * 1024


def _cparams(n_axes):
    return pltpu.CompilerParams(dimension_semantics=("arbitrary",) * n_axes,
                                vmem_limit_bytes=VMEM_LIMIT)


def _dot(a, b):
    return jnp.dot(a, b, preferred_element_type=F32)


def _dot_nt(a, b):
    return lax.dot_general(a, b, (((1,), (1,)), ((), ())), preferred_element_type=F32)


def _dot_tn(a, b):
    return lax.dot_general(a, b, (((0,), (0,)), ((), ())), preferred_element_type=F32)


def _split_bf16(x):
    hi = x.astype(BF16)
    lo = (x - hi.astype(F32)).astype(BF16)
    return hi, lo


def _diff_lambda(lam_ref):
    l4 = lam_ref[...]
    s1 = jnp.sum(l4[0:1] * l4[1:2], axis=-1, keepdims=True)
    s2 = jnp.sum(l4[2:3] * l4[3:4], axis=-1, keepdims=True)
    return jnp.exp(s1) - jnp.exp(s2) + LAMBDA_INIT


def _proj_kernel(x_ref, rope_ref, gmix_ref, wmain_ref, wlow_ref, wa2_ref, ba_ref, gq_ref, gk_ref,
                 gsum_ref, *out_refs, transposed, tq, tk):
    x = x_ref[0]
    tm = x.shape[0]
    ms = jnp.mean(x * x, axis=-1, keepdims=True)
    hn = (x * lax.rsqrt(ms + EPS) * gmix_ref[...]).astype(BF16)

    def seg(lo, hi):
        return _dot(hn, wmain_ref[:, lo:hi])

    cos = rope_ref[:, 0:LANES]
    sin_a = rope_ref[:, LANES:2 * LANES]
    sin_b = rope_ref[:, 2 * LANES:3 * LANES]
    gsum = gsum_ref[...]

    def qknorm_rope(p, g_ref):
        hi, lo = _split_bf16(p * p)
        ss = _dot(hi, gsum) + _dot(lo, gsum)
        pn = p * lax.rsqrt(ss * (1.0 / HEAD_DIM_A) + EPS) * g_ref[...]
        cols = []
        for c in range(QK_W_A // LANES):
            blk = pn[:, c * LANES:(c + 1) * LANES]
            cols.append(blk * cos + pltpu.roll(blk, LANES - ROT_DIM // 2, 1) * sin_a
                        + pltpu.roll(blk, ROT_DIM // 2, 1) * sin_b)
        return jnp.concatenate(cols, axis=1)

    o = 0
    qa = qknorm_rope(seg(o, o + QK_W_A), gq_ref); o += QK_W_A
    ka = qknorm_rope(seg(o, o + QK_W_A), gk_ref); o += QK_W_A
    va = seg(o, o + VAL_W_A); o += VAL_W_A
    qb = seg(o, o + KEY_W_B); o += KEY_W_B
    kb = seg(o, o + KEY_W_B); o += KEY_W_B
    vb = seg(o, o + VAL_W_B); o += VAL_W_B
    gb = seg(o, o + VAL_W_B); o += VAL_W_B
    a_low = _dot(hn, wlow_ref[...])
    gate = _dot(a_low.astype(BF16), wa2_ref[...]) + ba_ref[...]
    log_a = (jnp.minimum(gate, 0.0) - jnp.log(1.0 + jnp.exp(-jnp.abs(gate)))) * (1.0 / GATE_TAU)
    sg = gb * (1.0 / (1.0 + jnp.exp(-gb)))
    qa = qa * (HEAD_DIM_A ** -0.5 * LOG2E)
    qb = qb * (KEY_DIM_B ** -0.5)

    if transposed:
        qT_ref, k_ref, vT_ref, kf_ref, vf_ref, qb_ref, kb_ref, vb_ref, sg_ref, la_ref = out_refs
        for r in range(tm // tq):
            qT_ref[0, r] = qa[r * tq:(r + 1) * tq, :].T.astype(BF16)
        for r in range(tm // tk):
            vT_ref[0, r] = va[r * tk:(r + 1) * tk, :].T.astype(BF16)
        k_ref[0] = ka.astype(BF16)
    else:
        q_ref, kf_ref, vf_ref, qb_ref, kb_ref, vb_ref, sg_ref, la_ref = out_refs
        q_ref[0] = qa.astype(BF16)
    for h in range(N_HEADS_A):
        hs = slice(h * HEAD_V_A, (h + 1) * HEAD_V_A)
        kf_ref[0, pl.ds(h, tm, stride=N_HEADS_A), :] = ka[:, hs]
        vf_ref[0, pl.ds(h, tm, stride=N_HEADS_A), :] = va[:, hs]
    qb_ref[0] = qb.astype(BF16)
    kb_ref[0] = kb.astype(BF16)
    vb_ref[0] = vb.astype(BF16)
    sg_ref[0] = sg.astype(BF16)
    la_ref[0] = log_a


def _proj(x, rope, w, *, tm, transposed, tq=256, tk=256, lead_rows=0):
    nb, s, d = x.shape
    assert s % tm == 0
    grid = (s // tm, nb)

    def tok(width, dtype):
        return (jax.ShapeDtypeStruct((nb, s, width), dtype),
                pl.BlockSpec((1, tm, width), lambda i, b: (b, i, 0)))

    def const(a):
        return pl.BlockSpec(a.shape, lambda i, b: (0,) * a.ndim)

    outs = []
    if transposed:
        assert tm % tq == 0 and tm % tk == 0
        outs.append((jax.ShapeDtypeStruct((nb, s // tq, QK_W_A, tq), BF16),
                     pl.BlockSpec((1, tm // tq, QK_W_A, tq), lambda i, b: (b, i, 0, 0))))
        outs.append(tok(QK_W_A, BF16))
        outs.append((jax.ShapeDtypeStruct((nb, s // tk, VAL_W_A, tk), BF16),
                     pl.BlockSpec((1, tm // tk, VAL_W_A, tk), lambda i, b: (b, i, 0, 0))))
    else:
        outs.append(tok(QK_W_A, BF16))
    kv_rows = tm * N_HEADS_A
    kv_shape = jax.ShapeDtypeStruct((nb, (lead_rows + s) * N_HEADS_A, HEAD_V_A), F32)
    if lead_rows:
        kv_spec = pl.BlockSpec((pl.Element(1), pl.Element(kv_rows), pl.Element(HEAD_V_A)),
                               lambda i, b: (b, pl.multiple_of(
                                   lead_rows * N_HEADS_A + i * kv_rows, 8), 0))
    else:
        kv_spec = pl.BlockSpec((1, kv_rows, HEAD_V_A), lambda i, b: (b, i, 0))
    outs += [(kv_shape, kv_spec), (kv_shape, kv_spec), tok(KEY_W_B, BF16), tok(KEY_W_B, BF16),
             tok(VAL_W_B, BF16), tok(VAL_W_B, BF16), tok(KEY_W_B, F32)]
    consts = [w['gmix'], w['w_main'], w['w_low'], w['w_a2'], w['b_a'], w['gq'], w['gk'], w['gsum']]
    return pl.pallas_call(
        functools.partial(_proj_kernel, transposed=transposed, tq=tq, tk=tk),
        grid=grid,
        in_specs=[pl.BlockSpec((1, tm, d), lambda i, b: (b, i, 0)),
                  pl.BlockSpec((tm, 3 * LANES), lambda i, b: (i, 0))] + [const(a) for a in consts],
        out_specs=[o[1] for o in outs],
        out_shape=[o[0] for o in outs],
        compiler_params=_cparams(2),
        name="proj_T" if transposed else "proj_small",
    )(x, rope, *consts)


def _colmax(s):
    rows = s.shape[0]
    slabs = 8 if rows % 64 == 0 else 1
    if slabs > 1:
        s = jnp.max(s.reshape(slabs, rows // slabs, s.shape[1]), axis=0)
    return jnp.max(s, axis=0, keepdims=True)


def _attn_kernel(lam_ref, qT_ref, k_ref, vT_ref, km_ref, vmT_ref, g_ref, o_ref,
                 qz_sc, s_sc, m_sc, acc_sc, *, tq, tk, cw, n_meta):
    i = pl.program_id(2)
    qT = qT_ref[0, 0]
    row = lax.broadcasted_iota(jnp.int32, qT.shape, 0)
    zero = jnp.zeros_like(qT)
    qz_sc[:, :tq] = jnp.where(row < HEAD_DIM_A, qT, zero)
    qz_sc[:, tq:] = jnp.where(row >= HEAD_DIM_A, qT, zero)
    n_cb = 2 * tq // cw

    def ext(vt):
        return jnp.concatenate([vt, jnp.ones((16, vt.shape[1]), BF16)], axis=0)

    def scores(j, cb):
        kt = k_ref[0, pl.ds(pl.multiple_of(j * tk, tk), tk), :]
        return _dot(kt, qz_sc[:, cb * cw:(cb + 1) * cw])

    def update(cb, s, vt):
        cs = slice(cb * cw, (cb + 1) * cw)
        m_old = m_sc[:, cs]
        m_new = jnp.maximum(m_old, _colmax(s))
        alpha = jnp.exp2(m_old - m_new)
        p = jnp.exp2(s - m_new).astype(BF16)
        m_sc[:, cs] = m_new
        acc_sc[:, cs] = alpha * acc_sc[:, cs] + _dot(vt, p)

    s_meta = [_dot(km_ref[...], qz_sc[:, cb * cw:(cb + 1) * cw]) for cb in range(n_cb)]
    for cb in range(n_cb):
        s_sc[cb] = scores(0, cb)
    vm = ext(vmT_ref[0])
    for cb in range(n_cb):
        cs = slice(cb * cw, (cb + 1) * cw)
        krow = lax.broadcasted_iota(jnp.int32, s_meta[cb].shape, 0)
        s = jnp.where(krow < n_meta, s_meta[cb], NEG)
        m0 = jnp.max(s, axis=0, keepdims=True)
        m_sc[:, cs] = m0
        acc_sc[:, cs] = _dot(vm, jnp.exp2(s - m0).astype(BF16))

    def full_tile(j):
        vt = ext(vT_ref[0, j])
        for cb in range(n_cb):
            s = s_sc[cb]
            s_sc[cb] = scores(j + 1, cb)
            update(cb, s, vt)

    def body(jj, carry):
        full_tile(2 * jj)
        full_tile(2 * jj + 1)
        return carry

    lax.fori_loop(0, lax.shift_right_logical(i, 1), body, 0)

    @pl.when((i & 1) == 1)
    def _():
        full_tile(i - 1)

    for cb in range(n_cb):
        q0 = (cb * cw) % tq
        rows = min(tk, ((q0 + cw - 1) // CHUNK + 1) * CHUNK)
        s = s_sc[cb, 0:rows, :]
        kr = lax.broadcasted_iota(jnp.int32, s.shape, 0) // CHUNK
        qc = (lax.broadcasted_iota(jnp.int32, s.shape, 1) + q0) // CHUNK
        update(cb, jnp.where(kr <= qc, s, NEG), ext(vT_ref[0, i][:, :rows]))

    lam = _diff_lambda(lam_ref)
    o = acc_sc[0:HEAD_V_A, :] * (1.0 / acc_sc[HEAD_V_A:HEAD_V_A + 1, :])
    oT = o[:, :tq] - lam * o[:, tq:]
    ms = jnp.mean(oT * oT, axis=0, keepdims=True)
    on = (oT * lax.rsqrt(ms + EPS)).T
    o_ref[0] = (on * g_ref[...] * (1.0 - LAMBDA_INIT)).astype(BF16)


def _attn(lam4, qT, k, vT, km, vmT, g_diff, *, n_meta):
    nb, nq, _, tq = qT.shape
    _, nk, _, tk = vT.shape
    s = k.shape[1]
    assert tq == tk
    cw = min(ATTN_COL_BLOCK, tq)
    grid = (nb, N_HEADS_A, nq)
    return pl.pallas_call(
        functools.partial(_attn_kernel, tq=tq, tk=tk, cw=cw, n_meta=n_meta),
        grid=grid,
        in_specs=[pl.BlockSpec(lam4.shape, lambda b, h, i: (0, 0)),
                  pl.BlockSpec((1, 1, HEAD_V_A, tq), lambda b, h, i: (b, i, h, 0)),
                  pl.BlockSpec((1, s, HEAD_V_A), lambda b, h, i: (b, 0, h)),
                  pl.BlockSpec((1, nk, HEAD_V_A, tk), lambda b, h, i: (b, 0, h, 0)),
                  pl.BlockSpec((LANES, HEAD_V_A), lambda b, h, i: (0, h)),
                  pl.BlockSpec((1, HEAD_V_A, LANES), lambda b, h, i: (h, 0, 0)),
                  pl.BlockSpec((1, HEAD_V_A), lambda b, h, i: (0, h))],
        out_specs=pl.BlockSpec((1, tq, HEAD_V_A), lambda b, h, i: (b, i, h)),
        out_shape=jax.ShapeDtypeStruct((nb, s, VAL_W_A), BF16),
        scratch_shapes=[pltpu.VMEM((HEAD_V_A, 2 * tq), BF16),
                        pltpu.VMEM((2 * tq // cw, tk, cw), F32), pltpu.VMEM((1, 2 * tq), F32),
                        pltpu.VMEM((HEAD_V_A + 16, 2 * tq), F32)],
        compiler_params=_cparams(3),
        name="attn",
    )(lam4, qT, k, vT, km, vmT, g_diff)


def _sattn_kernel(lam_ref, q_ref, ck_ref, cv_ref, kn_ref, vn_ref, g_ref, o_ref, *, p_main, p_all):
    lam = _diff_lambda(lam_ref)
    lane = lax.broadcasted_iota(jnp.int32, (1, LANES), 1)
    for h in range(N_HEADS_A):
        hs = slice(h * HEAD_V_A, (h + 1) * HEAD_V_A)
        qh = q_ref[0, :, hs]
        t = qh.shape[0]

        def head_rows(ref, lo, n):
            return ref[0, pl.ds(lo * N_HEADS_A + h, n, stride=N_HEADS_A), :]

        ks = [head_rows(ck_ref, 0, p_main)]
        vs = [head_rows(cv_ref, 0, p_main)]
        if p_all > p_main:
            ks.append(jnp.concatenate([head_rows(ck_ref, p_main, p_all - p_main),
                                       head_rows(kn_ref, 0, t)], axis=0))
            vs.append(jnp.concatenate([head_rows(cv_ref, p_main, p_all - p_main),
                                       head_rows(vn_ref, 0, t)], axis=0))
        else:
            ks.append(head_rows(kn_ref, 0, t))
            vs.append(head_rows(vn_ref, 0, t))
        ks = [a.astype(BF16) for a in ks]
        vs = [a.astype(BF16) for a in vs]
        outs = []
        for c in range(2):
            sel = (lane >= c * HEAD_DIM_A) & (lane < (c + 1) * HEAD_DIM_A)
            qc = jnp.where(sel, qh, jnp.zeros_like(qh))
            ss = [_dot_nt(qc, kk) for kk in ks]
            m = functools.reduce(jnp.maximum, [jnp.max(a, axis=-1, keepdims=True) for a in ss])
            ps = [jnp.exp2(a - m) for a in ss]
            l = sum(jnp.sum(a, axis=-1, keepdims=True) for a in ps)
            acc = sum(_dot(a.astype(BF16), vv) for a, vv in zip(ps, vs))
            outs.append(acc * (1.0 / l))
        o = outs[0] - lam * outs[1]
        ms = jnp.mean(o * o, axis=-1, keepdims=True)
        y = o * lax.rsqrt(ms + EPS) * g_ref[:, hs] * (1.0 - LAMBDA_INIT)
        o_ref[0, :, hs] = y.astype(BF16)


def _sattn(lam4, q, ck, cv, kn, vn, g_diff):
    db, t, _ = q.shape
    p_all = ck.shape[1] // N_HEADS_A
    p_main = (p_all // LANES) * LANES
    assert (p_all - p_main + t) % 16 == 0

    def tokspec(n):
        return pl.BlockSpec((1, n, VAL_W_A), lambda b: (b, 0, 0))

    def rowspec(n):
        return pl.BlockSpec((1, n * N_HEADS_A, HEAD_V_A), lambda b: (b, 0, 0))

    return pl.pallas_call(
        functools.partial(_sattn_kernel, p_main=p_main, p_all=p_all),
        grid=(db,),
        in_specs=[pl.BlockSpec(lam4.shape, lambda b: (0, 0)), tokspec(t), rowspec(p_all),
                  rowspec(p_all), rowspec(t), rowspec(t),
                  pl.BlockSpec((1, VAL_W_A), lambda b: (0, 0))],
        out_specs=tokspec(t),
        out_shape=jax.ShapeDtypeStruct((db, t, VAL_W_A), BF16),
        compiler_params=_cparams(1),
        name="sattn",
    )(lam4, q, ck, cv, kn, vn, g_diff)


def _gla_kernel(q_ref, k_ref, v_ref, la_ref, sg_ref, s0_ref, g_ref, lbd_ref, y_ref, sout_ref,
                st_sc, o_sc, *, n_chunks):
    t = pl.program_id(1)
    n_pairs = N_HEADS_B // 2

    @pl.when(t == 0)
    def _():
        for p in range(n_pairs):
            st_sc[p] = s0_ref[0, p].T

    la = la_ref[0]
    hi, lo = _split_bf16(la)
    lbd = lbd_ref[...]
    b = _dot(lbd, hi) + _dot(lbd, lo)
    q = q_ref[0].astype(F32)
    k = k_ref[0].astype(F32)
    qf = q * jnp.exp(b)
    kdec = (k * jnp.exp(-b)).astype(BF16)
    lane = lax.broadcasted_iota(jnp.int32, (1, LANES), 1)
    tr = lax.broadcasted_iota(jnp.int32, (CHUNK, CHUNK), 0)
    tc = lax.broadcasted_iota(jnp.int32, (CHUNK, CHUNK), 1)
    causal = tc <= tr
    for c in range(n_chunks):
        rows = slice(c * CHUNK, (c + 1) * CHUNK)
        b_last = b[(c + 1) * CHUNK - 1:(c + 1) * CHUNK, :]
        decay = jnp.exp(b_last)
        kd2 = k[rows] * jnp.exp(b_last - b[rows])
        for p in range(n_pairs):
            ls = slice(p * LANES, (p + 1) * LANES)
            st = st_sc[p]
            st_bf = st.astype(BF16)
            qf_p = qf[rows, ls]
            kdec_p = kdec[rows, ls]
            kd2_p = kd2[:, ls]
            ds = jnp.zeros_like(st)
            for a in range(2):
                h = 2 * p + a
                sel = (lane >= a * KEY_DIM_B) & (lane < (a + 1) * KEY_DIM_B)
                q_h = jnp.where(sel, qf_p, 0.0).astype(BF16)
                sc = _dot_nt(q_h, kdec_p)
                sc = jnp.where(causal, sc, 0.0).astype(BF16)
                v_h = v_ref[0, rows, h * VAL_DIM_B:(h + 1) * VAL_DIM_B]
                o_sc[rows, h * VAL_DIM_B:(h + 1) * VAL_DIM_B] = _dot_nt(q_h, st_bf) + _dot(sc, v_h)
                kd2_h = jnp.where(sel, kd2_p, 0.0).astype(BF16)
                ds = ds + _dot_tn(v_h, kd2_h)
            st_sc[p] = st * decay[:, ls] + ds

    for h in range(N_HEADS_B):
        hs = slice(h * VAL_DIM_B, (h + 1) * VAL_DIM_B)
        o = o_sc[:, hs]
        ms = jnp.mean(o * o, axis=-1, keepdims=True)
        y = o * lax.rsqrt(ms + EPS) * g_ref[:, hs]
        y_ref[0, :, hs] = (y * sg_ref[0, :, hs].astype(F32)).astype(BF16)

    @pl.when(t == pl.num_programs(1) - 1)
    def _():
        for p in range(n_pairs):
            sout_ref[0, p] = st_sc[p].T


def _gla(q, k, v, la, sg, s0, g_gla, *, tg):
    nb, s, _ = q.shape
    assert s % tg == 0 and tg % CHUNK == 0
    n_chunks = tg // CHUNK
    s0p = s0.reshape(s0.shape[0], N_HEADS_B // 2, 2 * KEY_DIM_B, VAL_DIM_B)
    shared = s0p.shape[0] == 1
    idx = jnp.arange(tg)
    lbd = ((idx[:, None] // CHUNK == idx[None, :] // CHUNK)
           & (idx[None, :] <= idx[:, None])).astype(BF16)

    def tok(width):
        return pl.BlockSpec((1, tg, width), lambda b, t: (b, t, 0))

    sspec_in = pl.BlockSpec((1,) + s0p.shape[1:], (lambda b, t: (0, 0, 0, 0)) if shared
                            else (lambda b, t: (b, 0, 0, 0)))
    y, s_out = pl.pallas_call(
        functools.partial(_gla_kernel, n_chunks=n_chunks),
        grid=(nb, s // tg),
        in_specs=[tok(KEY_W_B), tok(KEY_W_B), tok(VAL_W_B), tok(KEY_W_B), tok(VAL_W_B), sspec_in,
                  pl.BlockSpec((1, VAL_W_B), lambda b, t: (0, 0)),
                  pl.BlockSpec((tg, tg), lambda b, t: (0, 0))],
        out_specs=[tok(VAL_W_B), pl.BlockSpec((1,) + s0p.shape[1:], lambda b, t: (b, 0, 0, 0))],
        out_shape=[jax.ShapeDtypeStruct((nb, s, VAL_W_B), BF16),
                   jax.ShapeDtypeStruct((nb,) + s0p.shape[1:], F32)],
        scratch_shapes=[pltpu.VMEM((N_HEADS_B // 2, VAL_DIM_B, 2 * KEY_DIM_B), F32),
                        pltpu.VMEM((tg, VAL_W_B), F32)],
        compiler_params=_cparams(2),
        name="gla",
    )(q, k, v, la, sg, s0p, g_gla, lbd)
    return y, s_out.reshape(nb, N_HEADS_B, KEY_DIM_B, VAL_DIM_B)


def _ffn_kernel(x_ref, ya_ref, yb_ref, woa_ref, wob_ref, gn_ref, wg_ref, wu_ref, wd_ref, o_ref,
                *, n_split):
    h = x_ref[...] + _dot(ya_ref[...], woa_ref[...]) + _dot(yb_ref[...], wob_ref[...])
    ms = jnp.mean(h * h, axis=-1, keepdims=True)
    hn = (h * lax.rsqrt(ms + EPS) * gn_ref[...]).astype(BF16)
    w = D_FF // n_split
    f = None
    for c in range(n_split):
        cs = slice(c * w, (c + 1) * w)
        g = _dot(hn, wg_ref[:, cs])
        u = _dot(hn, wu_ref[:, cs])
        a = (g * (1.0 / (1.0 + jnp.exp(-g))) * u).astype(BF16)
        d = _dot(a, wd_ref[cs, :])
        f = d if f is None else f + d
    o_ref[...] = h + f


def _ffn(x, ya, yb, w, *, tm, n_split=2):
    t, d = x.shape
    assert t % tm == 0

    def const(a):
        return pl.BlockSpec(a.shape, lambda i: (0,) * a.ndim)

    consts = [w['w_out_a'], w['w_out_b'], w['g_ffn'], w['w_gate'], w['w_up'], w['w_down']]
    return pl.pallas_call(
        functools.partial(_ffn_kernel, n_split=n_split),
        grid=(t // tm,),
        in_specs=[pl.BlockSpec((tm, d), lambda i: (i, 0)),
                  pl.BlockSpec((tm, VAL_W_A), lambda i: (i, 0)),
                  pl.BlockSpec((tm, VAL_W_B), lambda i: (i, 0))] + [const(a) for a in consts],
        out_specs=pl.BlockSpec((tm, d), lambda i: (i, 0)),
        out_shape=jax.ShapeDtypeStruct((t, d), F32),
        compiler_params=_cparams(1),
        name="ffn",
    )(x, ya, yb, *consts)


def _put_lead_kernel(k_any, v_any, km_ref, vm_ref, ko_ref, vo_ref):
    del k_any, v_any
    ko_ref[0] = km_ref[0]
    vo_ref[0] = vm_ref[0]


def _put_lead_rows(k, v, k_lead, v_lead):
    nb = k.shape[0]
    n_lead = k_lead.shape[1]
    lead = pl.BlockSpec((1, n_lead, HEAD_V_A), lambda b: (0, 0, 0))
    out = pl.BlockSpec((1, n_lead, HEAD_V_A), lambda b: (b, 0, 0))
    return pl.pallas_call(
        _put_lead_kernel,
        grid=(nb,),
        in_specs=[pl.BlockSpec(memory_space=pl.ANY), pl.BlockSpec(memory_space=pl.ANY), lead, lead],
        out_specs=[out, out],
        out_shape=[jax.ShapeDtypeStruct(k.shape, k.dtype), jax.ShapeDtypeStruct(v.shape, v.dtype)],
        input_output_aliases={0: 0, 1: 1},
        compiler_params=_cparams(1),
        name="put_lead",
    )(k, v, k_lead, v_lead)


def _rope_table(pos):
    half = ROT_DIM // 2
    inv_freq = ROPE_THETA ** (-jnp.arange(0, ROT_DIM, 2, dtype=F32) / ROT_DIM)
    ang = pos.astype(F32)[:, None] * inv_freq[None, :]
    cos, sin = jnp.cos(ang), jnp.sin(ang)
    lane = jnp.arange(LANES) % HEAD_DIM_A
    j = lane % half
    c = jnp.where(lane[None, :] < ROT_DIM, cos[:, j], 1.0)
    sa = jnp.where(lane[None, :] < half, -sin[:, j], 0.0)
    sb = jnp.where((lane[None, :] >= half) & (lane[None, :] < ROT_DIM), sin[:, j], 0.0)
    return jnp.concatenate([c, sa, sb], axis=1).astype(F32)


def kernel(x_prompt, x_sample, cache_k_diff, cache_v_diff, state_gla, meta_tokens, norm_mix, w_in,
           w_a2, b_a, q_norm, k_norm, lambda_q1, lambda_k1, lambda_q2, lambda_k2, g_diff, g_gla,
           w_out, norm_ffn, w_ffn_gate, w_ffn_up, w_ffn_down):
    depth = w_in.shape[0]
    assert depth == 1, "single-layer step"
    bsz, seq, d = x_prompt.shape
    db, dt, _ = x_sample.shape
    n_meta = meta_tokens.shape[0]
    past = cache_k_diff.shape[2]
    assert d == D_MODEL and n_meta <= LANES and seq % CHUNK == 0

    w_in0 = w_in[0]
    gid = jnp.arange(QK_W_A) // HEAD_DIM_A
    w = {
        'gmix': norm_mix[0][None, :],
        'w_main': w_in0[:, :N_MAIN].astype(BF16),
        'w_low': jnp.pad(w_in0[:, N_MAIN:], ((0, 0), (0, LANES - GATE_RANK))).astype(BF16),
        'w_a2': jnp.pad(w_a2[0], ((0, LANES - GATE_RANK), (0, 0))).astype(BF16),
        'b_a': b_a[0][None, :],
        'gq': jnp.tile(q_norm[0], QK_W_A // HEAD_DIM_A)[None, :],
        'gk': jnp.tile(k_norm[0], QK_W_A // HEAD_DIM_A)[None, :],
        'gsum': (gid[:, None] == gid[None, :]).astype(BF16),
        'w_out_a': w_out[0][:VAL_W_A].astype(BF16),
        'w_out_b': w_out[0][VAL_W_A:].astype(BF16),
        'g_ffn': norm_ffn[0][None, :],
        'w_gate': w_ffn_gate[0].astype(BF16),
        'w_up': w_ffn_up[0].astype(BF16),
        'w_down': w_ffn_down[0].astype(BF16),
    }
    lam4 = jnp.stack([lambda_q1[0], lambda_k1[0], lambda_q2[0], lambda_k2[0]])
    g_diff2 = g_diff[0][None, :]
    g_gla2 = g_gla[0][None, :]

    rope_m = _rope_table(jnp.arange(n_meta, dtype=jnp.int32))
    (_, kf_m, vf_m, qb_m, kb_m, vb_m, sg_m, la_m) = _proj(
        meta_tokens[None], rope_m, w, tm=n_meta, transposed=False)
    pad_m = ((0, 0), (0, CHUNK - n_meta), (0, 0))
    s_zero = jnp.zeros((1, N_HEADS_B, KEY_DIM_B, VAL_DIM_B), F32)
    _, s_meta = _gla(jnp.pad(qb_m, pad_m), jnp.pad(kb_m, pad_m), jnp.pad(vb_m, pad_m),
                     jnp.pad(la_m, pad_m), jnp.pad(sg_m, pad_m), s_zero, g_gla2, tg=CHUNK)

    tm = 512 if seq % 512 == 0 else seq
    tqk = 512 if seq % 512 == 0 else seq
    rope_x = _rope_table(n_meta + jnp.arange(seq, dtype=jnp.int32))
    (qT, k_bf, vT, kf_x, vf_x, qb_x, kb_x, vb_x, sg_x, la_x) = _proj(
        x_prompt, rope_x, w, tm=tm, transposed=True, tq=tqk, tk=tqk, lead_rows=n_meta)
    km = jnp.pad(kf_m.reshape(n_meta, QK_W_A), ((0, LANES - n_meta), (0, 0))).astype(BF16)
    vmT = jnp.pad(vf_m.reshape(n_meta, N_HEADS_A, HEAD_V_A).transpose(1, 2, 0),
                  ((0, 0), (0, 0), (0, LANES - n_meta))).astype(BF16)
    ya_x = _attn(lam4, qT, k_bf, vT, km, vmT, g_diff2, n_meta=n_meta)
    yb_x, s_final = _gla(qb_x, kb_x, vb_x, la_x, sg_x, s_meta, g_gla2, tg=tm)
    y_prompt = _ffn(x_prompt.reshape(bsz * seq, d), ya_x.reshape(bsz * seq, VAL_W_A),
                    yb_x.reshape(bsz * seq, VAL_W_B), w, tm=tm).reshape(bsz, seq, d)
    new_k_p, new_v_p = _put_lead_rows(kf_x, vf_x, kf_m, vf_m)

    ts = db * dt
    rope_s = jnp.tile(_rope_table(past + jnp.arange(dt, dtype=jnp.int32)), (db, 1))
    (q_s, kf_s, vf_s, qb_s, kb_s, vb_s, sg_s, la_s) = _proj(
        x_sample.reshape(1, ts, d), rope_s, w, tm=ts, transposed=False)
    ya_s = _sattn(lam4, q_s.reshape(db, dt, QK_W_A),
                  cache_k_diff.reshape(db, past * N_HEADS_A, HEAD_V_A),
                  cache_v_diff.reshape(db, past * N_HEADS_A, HEAD_V_A),
                  kf_s.reshape(db, dt * N_HEADS_A, HEAD_V_A),
                  vf_s.reshape(db, dt * N_HEADS_A, HEAD_V_A), g_diff2)
    pad_s = ((0, 0), (0, CHUNK - dt), (0, 0))

    def stream(a):
        return jnp.pad(a.reshape(db, dt, a.shape[-1]), pad_s)

    yb_s, s_new = _gla(stream(qb_s), stream(kb_s), stream(vb_s), stream(la_s), stream(sg_s),
                       state_gla[0], g_gla2, tg=CHUNK)
    y_sample = _ffn(x_sample.reshape(ts, d), ya_s.reshape(ts, VAL_W_A),
                    yb_s[:, :dt].reshape(ts, VAL_W_B), w, tm=ts).reshape(db, dt, d)

    return (y_prompt, y_sample,
            new_k_p.reshape(1, bsz, n_meta + seq, N_HEADS_A, 2 * HEAD_DIM_A),
            new_v_p.reshape(1, bsz, n_meta + seq, N_HEADS_A, HEAD_V_A),
            s_final[None],
            kf_s.reshape(1, db, dt, N_HEADS_A, 2 * HEAD_DIM_A),
            vf_s.reshape(1, db, dt, N_HEADS_A, HEAD_V_A),
            s_new[None])
```

```python
import functools
import math

import jax
import jax.numpy as jnp
from jax import lax
from jax.experimental import pallas as pl
from jax.experimental.pallas import tpu as pltpu

F32 = jnp.float32
BF16 = jnp.bfloat16

D_MODEL = 1024
N_HEADS_A = 4
HEAD_DIM_A = 64
HEAD_V_A = 128
QK_W_A = 512
VAL_W_A = 512
ROT_DIM = 16
ROPE_THETA = 500000.0
N_HEADS_B = 4
KEY_DIM_B = 64
VAL_DIM_B = 128
KEY_W_B = 256
VAL_W_B = 512
GATE_RANK = 16
GATE_TAU = 16.0
D_FF = 2816
EPS = 1e-6
CHUNK = 64
LAMBDA_INIT = 0.8 - 0.6 * math.exp(-0.3 * 0)
N_MAIN = 2 * QK_W_A + VAL_W_A + 2 * KEY_W_B + 2 * VAL_W_B
LANES = 128
NEG = -1e30
LOG2E = 1.4426950408889634
MXU_WIDTH = 256
ATTN_COL_BLOCK = MXU_WIDTH
ATTN_HEADS_PER_STEP = 2

VMEM_LIMIT = 56 * 1024 * 1024


def _cparams(n_axes):
    return pltpu.CompilerParams(dimension_semantics=("arbitrary",) * n_axes,
                                vmem_limit_bytes=VMEM_LIMIT)


def _dot(a, b):
    return jnp.dot(a, b, preferred_element_type=F32)


def _dot_nt(a, b):
    return lax.dot_general(a, b, (((1,), (1,)), ((), ())), preferred_element_type=F32)


def _dot_tn(a, b):
    return lax.dot_general(a, b, (((0,), (0,)), ((), ())), preferred_element_type=F32)


def _split_bf16(x):
    hi = x.astype(BF16)
    lo = (x - hi.astype(F32)).astype(BF16)
    return hi, lo


def _diff_lambda(lam_ref):
    l4 = lam_ref[...]
    s1 = jnp.sum(l4[0:1] * l4[1:2], axis=-1, keepdims=True)
    s2 = jnp.sum(l4[2:3] * l4[3:4], axis=-1, keepdims=True)
    return jnp.exp(s1) - jnp.exp(s2) + LAMBDA_INIT


def _proj_kernel(x_ref, rope_ref, gmix_ref, wmain_ref, wlow_ref, wa2_ref, ba_ref, gq_ref, gk_ref,
                 gsum_ref, *out_refs, transposed, tq, tk):
    x = x_ref[0]
    tm = x.shape[0]
    ms = jnp.mean(x * x, axis=-1, keepdims=True)
    hn = (x * lax.rsqrt(ms + EPS) * gmix_ref[...]).astype(BF16)

    def seg(lo, hi):
        return _dot(hn, wmain_ref[:, lo:hi])

    cos = rope_ref[:, 0:LANES]
    sin_a = rope_ref[:, LANES:2 * LANES]
    sin_b = rope_ref[:, 2 * LANES:3 * LANES]
    gsum = gsum_ref[...]

    def qknorm_rope(p, g_ref):
        p2 = (p * p).astype(BF16)
        wb = gsum.shape[0]
        ss = jnp.concatenate([_dot(p2[:, c * wb:(c + 1) * wb], gsum)
                              for c in range(QK_W_A // wb)], axis=1)
        pn = p * lax.rsqrt(ss * (1.0 / HEAD_DIM_A) + EPS) * g_ref[...]
        cols = []
        for c in range(QK_W_A // LANES):
            blk = pn[:, c * LANES:(c + 1) * LANES]
            cols.append(blk * cos + pltpu.roll(blk, LANES - ROT_DIM // 2, 1) * sin_a
                        + pltpu.roll(blk, ROT_DIM // 2, 1) * sin_b)
        return jnp.concatenate(cols, axis=1)

    o = 0
    qa = qknorm_rope(seg(o, o + QK_W_A), gq_ref); o += QK_W_A
    ka = qknorm_rope(seg(o, o + QK_W_A), gk_ref); o += QK_W_A
    va = seg(o, o + VAL_W_A); o += VAL_W_A
    qb = seg(o, o + KEY_W_B); o += KEY_W_B
    kb = seg(o, o + KEY_W_B); o += KEY_W_B
    vb = seg(o, o + VAL_W_B); o += VAL_W_B
    gb = seg(o, o + VAL_W_B); o += VAL_W_B
    a_low = _dot(hn, wlow_ref[...])
    gate = _dot(a_low.astype(BF16), wa2_ref[...]) + ba_ref[...]
    log_a = (jnp.minimum(gate, 0.0) - jnp.log(1.0 + jnp.exp(-jnp.abs(gate)))) * (1.0 / GATE_TAU)
    sg = gb * (1.0 / (1.0 + jnp.exp(-gb)))
    qa = qa * (HEAD_DIM_A ** -0.5 * LOG2E)
    qb = qb * (KEY_DIM_B ** -0.5)

    if transposed:
        qT_ref, k_ref, vT_ref, kf_ref, vf_ref, qb_ref, kb_ref, vb_ref, sg_ref, la_ref = out_refs
        for r in range(tm // tq):
            qT_ref[0, r] = qa[r * tq:(r + 1) * tq, :].T.astype(BF16)
        for r in range(tm // tk):
            vT_ref[0, r] = va[r * tk:(r + 1) * tk, :].T.astype(BF16)
        k_ref[0] = ka.astype(BF16)
    else:
        q_ref, kf_ref, vf_ref, qb_ref, kb_ref, vb_ref, sg_ref, la_ref = out_refs
        q_ref[0] = qa.astype(BF16)
    for h in range(N_HEADS_A):
        hs = slice(h * HEAD_V_A, (h + 1) * HEAD_V_A)
        kf_ref[0, pl.ds(h, tm, stride=N_HEADS_A), :] = ka[:, hs]
        vf_ref[0, pl.ds(h, tm, stride=N_HEADS_A), :] = va[:, hs]
    qb_ref[0] = qb.astype(BF16)
    kb_ref[0] = kb.astype(BF16)
    vb_ref[0] = vb.astype(BF16)
    sg_ref[0] = sg.astype(BF16)
    la_ref[0] = log_a


def _proj(x, rope, w, *, tm, transposed, tq=256, tk=256, lead_rows=0):
    nb, s, d = x.shape
    assert s % tm == 0
    grid = (s // tm, nb)

    def tok(width, dtype):
        return (jax.ShapeDtypeStruct((nb, s, width), dtype),
                pl.BlockSpec((1, tm, width), lambda i, b: (b, i, 0)))

    def const(a):
        return pl.BlockSpec(a.shape, lambda i, b: (0,) * a.ndim)

    outs = []
    if transposed:
        assert tm % tq == 0 and tm % tk == 0
        outs.append((jax.ShapeDtypeStruct((nb, s // tq, QK_W_A, tq), BF16),
                     pl.BlockSpec((1, tm // tq, QK_W_A, tq), lambda i, b: (b, i, 0, 0))))
        outs.append(tok(QK_W_A, BF16))
        outs.append((jax.ShapeDtypeStruct((nb, s // tk, VAL_W_A, tk), BF16),
                     pl.BlockSpec((1, tm // tk, VAL_W_A, tk), lambda i, b: (b, i, 0, 0))))
    else:
        outs.append(tok(QK_W_A, BF16))
    kv_rows = tm * N_HEADS_A
    kv_shape = jax.ShapeDtypeStruct((nb, (lead_rows + s) * N_HEADS_A, HEAD_V_A), F32)
    if lead_rows:
        kv_spec = pl.BlockSpec((pl.Element(1), pl.Element(kv_rows), pl.Element(HEAD_V_A)),
                               lambda i, b: (b, pl.multiple_of(
                                   lead_rows * N_HEADS_A + i * kv_rows, 8), 0))
    else:
        kv_spec = pl.BlockSpec((1, kv_rows, HEAD_V_A), lambda i, b: (b, i, 0))
    outs += [(kv_shape, kv_spec), (kv_shape, kv_spec), tok(KEY_W_B, BF16), tok(KEY_W_B, BF16),
             tok(VAL_W_B, BF16), tok(VAL_W_B, BF16), tok(KEY_W_B, F32)]
    consts = [w['gmix'], w['w_main'], w['w_low'], w['w_a2'], w['b_a'], w['gq'], w['gk'], w['gsum']]
    return pl.pallas_call(
        functools.partial(_proj_kernel, transposed=transposed, tq=tq, tk=tk),
        grid=grid,
        in_specs=[pl.BlockSpec((1, tm, d), lambda i, b: (b, i, 0)),
                  pl.BlockSpec((tm, 3 * LANES), lambda i, b: (i, 0))] + [const(a) for a in consts],
        out_specs=[o[1] for o in outs],
        out_shape=[o[0] for o in outs],
        compiler_params=_cparams(2),
        name="proj_T" if transposed else "proj_small",
    )(x, rope, *consts)


def _colmax(s):
    rows = s.shape[0]
    slabs = 8 if rows % 64 == 0 else 1
    if slabs > 1:
        s = jnp.max(s.reshape(slabs, rows // slabs, s.shape[1]), axis=0)
    return jnp.max(s, axis=0, keepdims=True)


def _attn_kernel(lam_ref, qT_ref, k_ref, vT_ref, km_ref, vmT_ref, g_ref, o_ref,
                 qz_sc, s_sc, m_sc, acc_sc, *, tq, tk, cw, n_meta, hps):
    i = pl.program_id(2)
    n_cb = 2 * tq // cw
    chains = [(hh, cb) for hh in range(hps) for cb in range(n_cb)]

    def hrows(hh):
        return slice(hh * HEAD_V_A, (hh + 1) * HEAD_V_A)

    for hh in range(hps):
        qT = qT_ref[0, 0, hrows(hh), :]
        row = lax.broadcasted_iota(jnp.int32, qT.shape, 0)
        zero = jnp.zeros_like(qT)
        qz_sc[hh, :, :tq] = jnp.where(row < HEAD_DIM_A, qT, zero)
        qz_sc[hh, :, tq:] = jnp.where(row >= HEAD_DIM_A, qT, zero)

    def ext(vt):
        return jnp.concatenate([vt, jnp.ones((16, vt.shape[1]), BF16)], axis=0)

    def scores(j, hh, cb):
        kt = k_ref[0, pl.ds(pl.multiple_of(j * tk, tk), tk), hrows(hh)]
        return _dot(kt, qz_sc[hh, :, cb * cw:(cb + 1) * cw])

    def update(hh, cb, s, vt):
        cs = slice(cb * cw, (cb + 1) * cw)
        m_old = m_sc[hh, :, cs]
        m_new = jnp.maximum(m_old, _colmax(s))
        alpha = jnp.exp2(m_old - m_new)
        p = jnp.exp2(s - m_new).astype(BF16)
        m_sc[hh, :, cs] = m_new
        acc_sc[hh, :, cs] = alpha * acc_sc[hh, :, cs] + _dot(vt, p)

    s_meta = [_dot(km_ref[:, hrows(hh)], qz_sc[hh, :, cb * cw:(cb + 1) * cw]) for hh, cb in chains]
    for n, (hh, cb) in enumerate(chains):
        s_sc[n] = scores(0, hh, cb)
    for n, (hh, cb) in enumerate(chains):
        cs = slice(cb * cw, (cb + 1) * cw)
        s = s_meta[n]
        m0 = jnp.max(s, axis=0, keepdims=True)
        p = jnp.concatenate([jnp.exp2(s - m0).astype(BF16),
                             jnp.zeros((LANES - n_meta, cw), BF16)], axis=0)
        m_sc[hh, :, cs] = m0
        acc_sc[hh, :, cs] = _dot(ext(vmT_ref[hh]), p)

    def full_tile(j):
        for n, (hh, cb) in enumerate(chains):
            s = s_sc[n]
            s_sc[n] = scores(j + 1, hh, cb)
            update(hh, cb, s, ext(vT_ref[0, j, hrows(hh), :]))

    def body(jj, carry):
        full_tile(2 * jj)
        full_tile(2 * jj + 1)
        return carry

    lax.fori_loop(0, lax.shift_right_logical(i, 1), body, 0)

    @pl.when((i & 1) == 1)
    def _():
        full_tile(i - 1)

    for n, (hh, cb) in enumerate(chains):
        q0 = (cb * cw) % tq
        rows = min(tk, ((q0 + cw - 1) // CHUNK + 1) * CHUNK)
        s = s_sc[n, 0:rows, :]
        kr = lax.broadcasted_iota(jnp.int32, s.shape, 0) // CHUNK
        qc = (lax.broadcasted_iota(jnp.int32, s.shape, 1) + q0) // CHUNK
        update(hh, cb, jnp.where(kr <= qc, s, NEG), ext(vT_ref[0, i, hrows(hh), 0:rows]))

    lam = _diff_lambda(lam_ref)
    for hh in range(hps):
        o = acc_sc[hh, 0:HEAD_V_A, :] * (1.0 / acc_sc[hh, HEAD_V_A:HEAD_V_A + 1, :])
        oT = o[:, :tq] - lam * o[:, tq:]
        ms = jnp.mean(oT * oT, axis=0, keepdims=True)
        on = (oT * lax.rsqrt(ms + EPS)).T
        o_ref[0, :, hrows(hh)] = (on * g_ref[:, hrows(hh)] * (1.0 - LAMBDA_INIT)).astype(BF16)


def _attn(lam4, qT, k, vT, km, vmT, g_diff, *, n_meta):
    nb, nq, _, tq = qT.shape
    _, nk, _, tk = vT.shape
    s = k.shape[1]
    assert tq == tk and n_meta % 16 == 0
    cw = min(ATTN_COL_BLOCK, tq)
    hps = ATTN_HEADS_PER_STEP
    hw = hps * HEAD_V_A
    grid = (nb, N_HEADS_A // hps, nq)
    return pl.pallas_call(
        functools.partial(_attn_kernel, tq=tq, tk=tk, cw=cw, n_meta=n_meta, hps=hps),
        grid=grid,
        in_specs=[pl.BlockSpec(lam4.shape, lambda b, g, i: (0, 0)),
                  pl.BlockSpec((1, 1, hw, tq), lambda b, g, i: (b, i, g, 0)),
                  pl.BlockSpec((1, s, hw), lambda b, g, i: (b, 0, g)),
                  pl.BlockSpec((1, nk, hw, tk), lambda b, g, i: (b, 0, g, 0)),
                  pl.BlockSpec((n_meta, hw), lambda b, g, i: (0, g)),
                  pl.BlockSpec((hps, HEAD_V_A, LANES), lambda b, g, i: (g, 0, 0)),
                  pl.BlockSpec((1, hw), lambda b, g, i: (0, g))],
        out_specs=pl.BlockSpec((1, tq, hw), lambda b, g, i: (b, i, g)),
        out_shape=jax.ShapeDtypeStruct((nb, s, VAL_W_A), BF16),
        scratch_shapes=[pltpu.VMEM((hps, HEAD_V_A, 2 * tq), BF16),
                        pltpu.VMEM((hps * 2 * tq // cw, tk, cw), F32),
                        pltpu.VMEM((hps, 1, 2 * tq), F32),
                        pltpu.VMEM((hps, HEAD_V_A + 16, 2 * tq), F32)],
        compiler_params=_cparams(3),
        name="attn",
    )(lam4, qT, k, vT, km, vmT, g_diff)


def _sattn_kernel(lam_ref, q_ref, ck_ref, cv_ref, kn_ref, vn_ref, g_ref, o_ref, *, p_main, p_all):
    lam = _diff_lambda(lam_ref)
    lane = lax.broadcasted_iota(jnp.int32, (1, LANES), 1)
    for h in range(N_HEADS_A):
        hs = slice(h * HEAD_V_A, (h + 1) * HEAD_V_A)
        qh = q_ref[0, :, hs]
        t = qh.shape[0]

        def head_rows(ref, lo, n):
            return ref[0, pl.ds(lo * N_HEADS_A + h, n, stride=N_HEADS_A), :]

        ks = [head_rows(ck_ref, 0, p_main)]
        vs = [head_rows(cv_ref, 0, p_main)]
        if p_all > p_main:
            ks.append(jnp.concatenate([head_rows(ck_ref, p_main, p_all - p_main),
                                       head_rows(kn_ref, 0, t)], axis=0))
            vs.append(jnp.concatenate([head_rows(cv_ref, p_main, p_all - p_main),
                                       head_rows(vn_ref, 0, t)], axis=0))
        else:
            ks.append(head_rows(kn_ref, 0, t))
            vs.append(head_rows(vn_ref, 0, t))
        ks = [a.astype(BF16) for a in ks]
        vs = [a.astype(BF16) for a in vs]
        outs = []
        for c in range(2):
            sel = (lane >= c * HEAD_DIM_A) & (lane < (c + 1) * HEAD_DIM_A)
            qc = jnp.where(sel, qh, jnp.zeros_like(qh))
            ss = [_dot_nt(qc, kk) for kk in ks]
            m = functools.reduce(jnp.maximum, [jnp.max(a, axis=-1, keepdims=True) for a in ss])
            ps = [jnp.exp2(a - m) for a in ss]
            l = sum(jnp.sum(a, axis=-1, keepdims=True) for a in ps)
            acc = sum(_dot(a.astype(BF16), vv) for a, vv in zip(ps, vs))
            outs.append(acc * (1.0 / l))
        o = outs[0] - lam * outs[1]
        ms = jnp.mean(o * o, axis=-1, keepdims=True)
        y = o * lax.rsqrt(ms + EPS) * g_ref[:, hs] * (1.0 - LAMBDA_INIT)
        o_ref[0, :, hs] = y.astype(BF16)


def _sattn(lam4, q, ck, cv, kn, vn, g_diff):
    db, t, _ = q.shape
    p_all = ck.shape[1] // N_HEADS_A
    p_main = (p_all // LANES) * LANES
    assert (p_all - p_main + t) % 16 == 0

    def tokspec(n):
        return pl.BlockSpec((1, n, VAL_W_A), lambda b: (b, 0, 0))

    def rowspec(n):
        return pl.BlockSpec((1, n * N_HEADS_A, HEAD_V_A), lambda b: (b, 0, 0))

    return pl.pallas_call(
        functools.partial(_sattn_kernel, p_main=p_main, p_all=p_all),
        grid=(db,),
        in_specs=[pl.BlockSpec(lam4.shape, lambda b: (0, 0)), tokspec(t), rowspec(p_all),
                  rowspec(p_all), rowspec(t), rowspec(t),
                  pl.BlockSpec((1, VAL_W_A), lambda b: (0, 0))],
        out_specs=tokspec(t),
        out_shape=jax.ShapeDtypeStruct((db, t, VAL_W_A), BF16),
        compiler_params=_cparams(1),
        name="sattn",
    )(lam4, q, ck, cv, kn, vn, g_diff)


def _gla_kernel(q_ref, k_ref, v_ref, la_ref, sg_ref, s0_ref, g_ref, lbd_ref, y_ref, sout_ref,
                st_sc, o_sc, *, n_chunks):
    t = pl.program_id(1)
    n_pairs = N_HEADS_B // 2

    @pl.when(t == 0)
    def _():
        for p in range(n_pairs):
            st_sc[p] = s0_ref[0, p].T

    la = la_ref[0]
    hi, lo = _split_bf16(la)
    lbd = lbd_ref[...]
    b = _dot(lbd, hi) + _dot(lbd, lo)
    q = q_ref[0].astype(F32)
    k = k_ref[0].astype(F32)
    qf = q * jnp.exp(b)
    kdec = (k * jnp.exp(-b)).astype(BF16)
    first = lax.broadcasted_iota(jnp.int32, (1, LANES), 1) < KEY_DIM_B
    tr = lax.broadcasted_iota(jnp.int32, (2 * CHUNK, CHUNK), 0) & (CHUNK - 1)
    tc = lax.broadcasted_iota(jnp.int32, (2 * CHUNK, CHUNK), 1)
    causal = tc <= tr
    blocks = [(c, p) for c in range(n_chunks) for p in range(n_pairs)]

    def rows_of(c):
        return slice(c * CHUNK, (c + 1) * CHUNK)

    def lanes_of(p):
        return slice(p * LANES, (p + 1) * LANES)

    def stack_heads(x):
        return jnp.concatenate([jnp.where(first, x, 0.0), jnp.where(first, 0.0, x)], axis=0)

    qm, sc = {}, {}
    for c, p in blocks:
        qm[c, p] = stack_heads(qf[rows_of(c), lanes_of(p)]).astype(BF16)
        sc[c, p] = _dot_nt(qm[c, p], kdec[rows_of(c), lanes_of(p)])
    ds, decay = {}, {}
    for c in range(n_chunks):
        b_last = b[(c + 1) * CHUNK - 1:(c + 1) * CHUNK, :]
        decay[c] = jnp.exp(b_last)
        kd2 = k[rows_of(c)] * jnp.exp(b_last - b[rows_of(c)])
        for p in range(n_pairs):
            v2 = jnp.concatenate([v_ref[0, rows_of(c), (2 * p + a) * VAL_DIM_B:(2 * p + a + 1) * VAL_DIM_B]
                                  for a in range(2)], axis=0)
            ds[c, p] = _dot_tn(v2, stack_heads(kd2[:, lanes_of(p)]).astype(BF16))
    st_in = {}
    for p in range(n_pairs):
        st = st_sc[p]
        for c in range(n_chunks):
            st_in[c, p] = st.astype(BF16)
            st = st * decay[c][:, lanes_of(p)] + ds[c, p]
        st_sc[p] = st
    for c, p in blocks:
        o_inter = _dot_nt(qm[c, p], st_in[c, p])
        s_cp = jnp.where(causal, sc[c, p], 0.0).astype(BF16)
        o_intra = _dot(s_cp, v_ref[0, rows_of(c), 2 * p * VAL_DIM_B:(2 * p + 2) * VAL_DIM_B])
        for a in range(2):
            h = 2 * p + a
            o_sc[rows_of(c), h * VAL_DIM_B:(h + 1) * VAL_DIM_B] = (
                o_inter[a * CHUNK:(a + 1) * CHUNK]
                + o_intra[a * CHUNK:(a + 1) * CHUNK, a * VAL_DIM_B:(a + 1) * VAL_DIM_B])

    for h in range(N_HEADS_B):
        hs = slice(h * VAL_DIM_B, (h + 1) * VAL_DIM_B)
        o = o_sc[:, hs]
        ms = jnp.mean(o * o, axis=-1, keepdims=True)
        y = o * lax.rsqrt(ms + EPS) * g_ref[:, hs]
        y_ref[0, :, hs] = (y * sg_ref[0, :, hs].astype(F32)).astype(BF16)

    @pl.when(t == pl.num_programs(1) - 1)
    def _():
        for p in range(n_pairs):
            sout_ref[0, p] = st_sc[p].T


def _gla(q, k, v, la, sg, s0, g_gla, *, tg):
    nb, s, _ = q.shape
    assert s % tg == 0 and tg % CHUNK == 0
    n_chunks = tg // CHUNK
    s0p = s0.reshape(s0.shape[0], N_HEADS_B // 2, 2 * KEY_DIM_B, VAL_DIM_B)
    shared = s0p.shape[0] == 1
    idx = jnp.arange(tg)
    lbd = ((idx[:, None] // CHUNK == idx[None, :] // CHUNK)
           & (idx[None, :] <= idx[:, None])).astype(BF16)

    def tok(width):
        return pl.BlockSpec((1, tg, width), lambda b, t: (b, t, 0))

    sspec_in = pl.BlockSpec((1,) + s0p.shape[1:], (lambda b, t: (0, 0, 0, 0)) if shared
                            else (lambda b, t: (b, 0, 0, 0)))
    y, s_out = pl.pallas_call(
        functools.partial(_gla_kernel, n_chunks=n_chunks),
        grid=(nb, s // tg),
        in_specs=[tok(KEY_W_B), tok(KEY_W_B), tok(VAL_W_B), tok(KEY_W_B), tok(VAL_W_B), sspec_in,
                  pl.BlockSpec((1, VAL_W_B), lambda b, t: (0, 0)),
                  pl.BlockSpec((tg, tg), lambda b, t: (0, 0))],
        out_specs=[tok(VAL_W_B), pl.BlockSpec((1,) + s0p.shape[1:], lambda b, t: (b, 0, 0, 0))],
        out_shape=[jax.ShapeDtypeStruct((nb, s, VAL_W_B), BF16),
                   jax.ShapeDtypeStruct((nb,) + s0p.shape[1:], F32)],
        scratch_shapes=[pltpu.VMEM((N_HEADS_B // 2, VAL_DIM_B, 2 * KEY_DIM_B), F32),
                        pltpu.VMEM((tg, VAL_W_B), F32)],
        compiler_params=_cparams(2),
        name="gla",
    )(q, k, v, la, sg, s0p, g_gla, lbd)
    return y, s_out.reshape(nb, N_HEADS_B, KEY_DIM_B, VAL_DIM_B)


def _ffn_kernel(x_ref, ya_ref, yb_ref, woa_ref, wob_ref, gn_ref, wg_ref, wu_ref, wd_ref, o_ref,
                *, n_split):
    h = x_ref[...] + _dot(ya_ref[...], woa_ref[...]) + _dot(yb_ref[...], wob_ref[...])
    ms = jnp.mean(h * h, axis=-1, keepdims=True)
    hn = (h * lax.rsqrt(ms + EPS) * gn_ref[...]).astype(BF16)
    tiles = D_FF // MXU_WIDTH
    bounds = [MXU_WIDTH * ((tiles * c + n_split - 1) // n_split) for c in range(n_split + 1)]
    f = None
    for c in range(n_split):
        cs = slice(bounds[c], bounds[c + 1])
        g = _dot(hn, wg_ref[:, cs])
        u = _dot(hn, wu_ref[:, cs])
        a = (g * (1.0 / (1.0 + jnp.exp(-g))) * u).astype(BF16)
        d = _dot(a, wd_ref[cs, :])
        f = d if f is None else f + d
    o_ref[...] = h + f


def _ffn(x, ya, yb, w, *, tm, n_split=2):
    t, d = x.shape
    assert t % tm == 0

    def const(a):
        return pl.BlockSpec(a.shape, lambda i: (0,) * a.ndim)

    consts = [w['w_out_a'], w['w_out_b'], w['g_ffn'], w['w_gate'], w['w_up'], w['w_down']]
    return pl.pallas_call(
        functools.partial(_ffn_kernel, n_split=n_split),
        grid=(t // tm,),
        in_specs=[pl.BlockSpec((tm, d), lambda i: (i, 0)),
                  pl.BlockSpec((tm, VAL_W_A), lambda i: (i, 0)),
                  pl.BlockSpec((tm, VAL_W_B), lambda i: (i, 0))] + [const(a) for a in consts],
        out_specs=pl.BlockSpec((tm, d), lambda i: (i, 0)),
        out_shape=jax.ShapeDtypeStruct((t, d), F32),
        compiler_params=_cparams(1),
        name="ffn",
    )(x, ya, yb, *consts)


def _put_lead_kernel(k_any, v_any, km_ref, vm_ref, ko_ref, vo_ref):
    del k_any, v_any
    ko_ref[0] = km_ref[0]
    vo_ref[0] = vm_ref[0]


def _put_lead_rows(k, v, k_lead, v_lead):
    nb = k.shape[0]
    n_lead = k_lead.shape[1]
    lead = pl.BlockSpec((1, n_lead, HEAD_V_A), lambda b: (0, 0, 0))
    out = pl.BlockSpec((1, n_lead, HEAD_V_A), lambda b: (b, 0, 0))
    return pl.pallas_call(
        _put_lead_kernel,
        grid=(nb,),
        in_specs=[pl.BlockSpec(memory_space=pl.ANY), pl.BlockSpec(memory_space=pl.ANY), lead, lead],
        out_specs=[out, out],
        out_shape=[jax.ShapeDtypeStruct(k.shape, k.dtype), jax.ShapeDtypeStruct(v.shape, v.dtype)],
        input_output_aliases={0: 0, 1: 1},
        compiler_params=_cparams(1),
        name="put_lead",
    )(k, v, k_lead, v_lead)


def _rope_table(pos):
    half = ROT_DIM // 2
    inv_freq = ROPE_THETA ** (-jnp.arange(0, ROT_DIM, 2, dtype=F32) / ROT_DIM)
    ang = pos.astype(F32)[:, None] * inv_freq[None, :]
    cos, sin = jnp.cos(ang), jnp.sin(ang)
    lane = jnp.arange(LANES) % HEAD_DIM_A
    j = lane % half
    c = jnp.where(lane[None, :] < ROT_DIM, cos[:, j], 1.0)
    sa = jnp.where(lane[None, :] < half, -sin[:, j], 0.0)
    sb = jnp.where((lane[None, :] >= half) & (lane[None, :] < ROT_DIM), sin[:, j], 0.0)
    return jnp.concatenate([c, sa, sb], axis=1).astype(F32)


def kernel(x_prompt, x_sample, cache_k_diff, cache_v_diff, state_gla, meta_tokens, norm_mix, w_in,
           w_a2, b_a, q_norm, k_norm, lambda_q1, lambda_k1, lambda_q2, lambda_k2, g_diff, g_gla,
           w_out, norm_ffn, w_ffn_gate, w_ffn_up, w_ffn_down):
    depth = w_in.shape[0]
    assert depth == 1, "single-layer step"
    bsz, seq, d = x_prompt.shape
    db, dt, _ = x_sample.shape
    n_meta = meta_tokens.shape[0]
    past = cache_k_diff.shape[2]
    assert d == D_MODEL and n_meta <= LANES and seq % CHUNK == 0

    w_in0 = w_in[0]
    gid = jnp.arange(MXU_WIDTH) // HEAD_DIM_A
    w = {
        'gmix': norm_mix[0][None, :],
        'w_main': w_in0[:, :N_MAIN].astype(BF16),
        'w_low': jnp.pad(w_in0[:, N_MAIN:], ((0, 0), (0, LANES - GATE_RANK))).astype(BF16),
        'w_a2': jnp.pad(w_a2[0], ((0, LANES - GATE_RANK), (0, 0))).astype(BF16),
        'b_a': b_a[0][None, :],
        'gq': jnp.tile(q_norm[0], QK_W_A // HEAD_DIM_A)[None, :],
        'gk': jnp.tile(k_norm[0], QK_W_A // HEAD_DIM_A)[None, :],
        'gsum': (gid[:, None] == gid[None, :]).astype(BF16),
        'w_out_a': w_out[0][:VAL_W_A].astype(BF16),
        'w_out_b': w_out[0][VAL_W_A:].astype(BF16),
        'g_ffn': norm_ffn[0][None, :],
        'w_gate': w_ffn_gate[0].astype(BF16),
        'w_up': w_ffn_up[0].astype(BF16),
        'w_down': w_ffn_down[0].astype(BF16),
    }
    lam4 = jnp.stack([lambda_q1[0], lambda_k1[0], lambda_q2[0], lambda_k2[0]])
    g_diff2 = g_diff[0][None, :]
    g_gla2 = g_gla[0][None, :]

    rope_m = _rope_table(jnp.arange(n_meta, dtype=jnp.int32))
    (_, kf_m, vf_m, qb_m, kb_m, vb_m, sg_m, la_m) = _proj(
        meta_tokens[None], rope_m, w, tm=n_meta, transposed=False)
    pad_m = ((0, 0), (0, CHUNK - n_meta), (0, 0))
    s_zero = jnp.zeros((1, N_HEADS_B, KEY_DIM_B, VAL_DIM_B), F32)
    _, s_meta = _gla(jnp.pad(qb_m, pad_m), jnp.pad(kb_m, pad_m), jnp.pad(vb_m, pad_m),
                     jnp.pad(la_m, pad_m), jnp.pad(sg_m, pad_m), s_zero, g_gla2, tg=CHUNK)

    tm = 512 if seq % 512 == 0 else seq
    tqk = 512 if seq % 512 == 0 else seq
    rope_x = _rope_table(n_meta + jnp.arange(seq, dtype=jnp.int32))
    (qT, k_bf, vT, kf_x, vf_x, qb_x, kb_x, vb_x, sg_x, la_x) = _proj(
        x_prompt, rope_x, w, tm=tm, transposed=True, tq=tqk, tk=tqk, lead_rows=n_meta)
    km = kf_m.reshape(n_meta, QK_W_A).astype(BF16)
    vmT = jnp.pad(vf_m.reshape(n_meta, N_HEADS_A, HEAD_V_A).transpose(1, 2, 0),
                  ((0, 0), (0, 0), (0, LANES - n_meta))).astype(BF16)
    ya_x = _attn(lam4, qT, k_bf, vT, km, vmT, g_diff2, n_meta=n_meta)
    yb_x, s_final = _gla(qb_x, kb_x, vb_x, la_x, sg_x, s_meta, g_gla2, tg=tm)
    y_prompt = _ffn(x_prompt.reshape(bsz * seq, d), ya_x.reshape(bsz * seq, VAL_W_A),
                    yb_x.reshape(bsz * seq, VAL_W_B), w, tm=tm).reshape(bsz, seq, d)
    new_k_p, new_v_p = _put_lead_rows(kf_x, vf_x, kf_m, vf_m)

    ts = db * dt
    rope_s = jnp.tile(_rope_table(past + jnp.arange(dt, dtype=jnp.int32)), (db, 1))
    (q_s, kf_s, vf_s, qb_s, kb_s, vb_s, sg_s, la_s) = _proj(
        x_sample.reshape(1, ts, d), rope_s, w, tm=ts, transposed=False)
    ya_s = _sattn(lam4, q_s.reshape(db, dt, QK_W_A),
                  cache_k_diff.reshape(db, past * N_HEADS_A, HEAD_V_A),
                  cache_v_diff.reshape(db, past * N_HEADS_A, HEAD_V_A),
                  kf_s.reshape(db, dt * N_HEADS_A, HEAD_V_A),
                  vf_s.reshape(db, dt * N_HEADS_A, HEAD_V_A), g_diff2)
    pad_s = ((0, 0), (0, CHUNK - dt), (0, 0))

    def stream(a):
        return jnp.pad(a.reshape(db, dt, a.shape[-1]), pad_s)

    yb_s, s_new = _gla(stream(qb_s), stream(kb_s), stream(vb_s), stream(la_s), stream(sg_s),
                       state_gla[0], g_gla2, tg=CHUNK)
    y_sample = _ffn(x_sample.reshape(ts, d), ya_s.reshape(ts, VAL_W_A),
                    yb_s[:, :dt].reshape(ts, VAL_W_B), w, tm=ts).reshape(db, dt, d)

    return (y_prompt, y_sample,
            new_k_p.reshape(1, bsz, n_meta + seq, N_HEADS_A, 2 * HEAD_DIM_A),
            new_v_p.reshape(1, bsz, n_meta + seq, N_HEADS_A, HEAD_V_A),
            s_final[None],
            kf_s.reshape(1, db, dt, N_HEADS_A, 2 * HEAD_DIM_A),
            vf_s.reshape(1, db, dt, N_HEADS_A, HEAD_V_A),
            s_new[None])
```

```python
import functools
import math

import jax
import jax.numpy as jnp
from jax import lax
from jax.experimental import pallas as pl
from jax.experimental.pallas import tpu as pltpu

F32 = jnp.float32
BF16 = jnp.bfloat16

D_MODEL = 1024
N_HEADS_A = 4
HEAD_DIM_A = 64
HEAD_V_A = 128
QK_W_A = 512
VAL_W_A = 512
ROT_DIM = 16
ROPE_THETA = 500000.0
N_HEADS_B = 4
KEY_DIM_B = 64
VAL_DIM_B = 128
KEY_W_B = 256
VAL_W_B = 512
GATE_RANK = 16
GATE_TAU = 16.0
D_FF = 2816
EPS = 1e-6
CHUNK = 64
LAMBDA_INIT = 0.8 - 0.6 * math.exp(-0.3 * 0)
N_MAIN = 2 * QK_W_A + VAL_W_A + 2 * KEY_W_B + 2 * VAL_W_B
LANES = 128
NEG = -1e30
LOG2E = 1.4426950408889634
MXU_WIDTH = 256
ATTN_COL_BLOCK = MXU_WIDTH
ATTN_HEADS_PER_STEP = 4

VMEM_LIMIT = 56 * 1024 * 1024


def _cparams(n_axes):
    return pltpu.CompilerParams(dimension_semantics=("arbitrary",) * n_axes,
                                vmem_limit_bytes=VMEM_LIMIT)


def _dot(a, b):
    return jnp.dot(a, b, preferred_element_type=F32)


def _dot_nt(a, b):
    return lax.dot_general(a, b, (((1,), (1,)), ((), ())), preferred_element_type=F32)


def _dot_tn(a, b):
    return lax.dot_general(a, b, (((0,), (0,)), ((), ())), preferred_element_type=F32)


def _split_bf16(x):
    hi = x.astype(BF16)
    lo = (x - hi.astype(F32)).astype(BF16)
    return hi, lo


def _diff_lambda(lam_ref):
    l4 = lam_ref[...]
    s1 = jnp.sum(l4[0:1] * l4[1:2], axis=-1, keepdims=True)
    s2 = jnp.sum(l4[2:3] * l4[3:4], axis=-1, keepdims=True)
    return jnp.exp(s1) - jnp.exp(s2) + LAMBDA_INIT


def _proj_kernel(x_ref, rope_ref, gmix_ref, wmain_ref, wlow_ref, wa2_ref, ba_ref, gq_ref, gk_ref,
                 gsum_ref, *out_refs, transposed, tq, tk):
    x = x_ref[0]
    tm = x.shape[0]
    ms = jnp.mean(x * x, axis=-1, keepdims=True)
    hn = (x * lax.rsqrt(ms + EPS) * gmix_ref[...]).astype(BF16)

    def seg(lo, hi):
        return _dot(hn, wmain_ref[:, lo:hi])

    cos = rope_ref[:, 0:LANES]
    sin_a = rope_ref[:, LANES:2 * LANES]
    sin_b = rope_ref[:, 2 * LANES:3 * LANES]
    gsum = gsum_ref[...]

    def qknorm_rope(p, g_ref):
        p2 = (p * p).astype(BF16)
        wb = gsum.shape[0]
        ss = jnp.concatenate([_dot(p2[:, c * wb:(c + 1) * wb], gsum)
                              for c in range(QK_W_A // wb)], axis=1)
        pn = p * lax.rsqrt(ss * (1.0 / HEAD_DIM_A) + EPS) * g_ref[...]
        cols = []
        for c in range(QK_W_A // LANES):
            blk = pn[:, c * LANES:(c + 1) * LANES]
            cols.append(blk * cos + pltpu.roll(blk, LANES - ROT_DIM // 2, 1) * sin_a
                        + pltpu.roll(blk, ROT_DIM // 2, 1) * sin_b)
        return jnp.concatenate(cols, axis=1)

    o_qa, o_ka, o_va = 0, QK_W_A, 2 * QK_W_A
    o_qb = o_va + VAL_W_A
    o_kb, o_vb = o_qb + KEY_W_B, o_qb + 2 * KEY_W_B
    o_gb = o_vb + VAL_W_B
    qa = qknorm_rope(seg(o_qa, o_qa + QK_W_A), gq_ref)
    qa = qa * (HEAD_DIM_A ** -0.5 * LOG2E)
    ka = qknorm_rope(seg(o_ka, o_ka + QK_W_A), gk_ref)
    va = seg(o_va, o_va + VAL_W_A)
    qb = seg(o_qb, o_qb + KEY_W_B) * (KEY_DIM_B ** -0.5)
    kb = seg(o_kb, o_kb + KEY_W_B)
    vb = seg(o_vb, o_vb + VAL_W_B)
    gb = seg(o_gb, o_gb + VAL_W_B)
    sg = gb * (1.0 / (1.0 + jnp.exp(-gb)))
    a_low = _dot(hn, wlow_ref[...])
    gate = _dot(a_low.astype(BF16), wa2_ref[...]) + ba_ref[...]
    log_a = (jnp.minimum(gate, 0.0) - jnp.log(1.0 + jnp.exp(-jnp.abs(gate)))) * (1.0 / GATE_TAU)

    if transposed:
        qT_ref, k_ref, vT_ref, kf_ref, vf_ref, qb_ref, kb_ref, vb_ref, sg_ref, la_ref = out_refs
        for r in range(tm // tq):
            qT_ref[0, r] = qa[r * tq:(r + 1) * tq, :].T.astype(BF16)
        for r in range(tm // tk):
            vT_ref[0, r] = va[r * tk:(r + 1) * tk, :].T.astype(BF16)
        k_ref[0] = ka.astype(BF16)
    else:
        q_ref, kf_ref, vf_ref, qb_ref, kb_ref, vb_ref, sg_ref, la_ref = out_refs
        q_ref[0] = qa.astype(BF16)
    for h in range(N_HEADS_A):
        hs = slice(h * HEAD_V_A, (h + 1) * HEAD_V_A)
        kf_ref[0, pl.ds(h, tm, stride=N_HEADS_A), :] = ka[:, hs]
        vf_ref[0, pl.ds(h, tm, stride=N_HEADS_A), :] = va[:, hs]
    qb_ref[0] = qb.astype(BF16)
    kb_ref[0] = kb.astype(BF16)
    vb_ref[0] = vb.astype(BF16)
    sg_ref[0] = sg.astype(BF16)
    la_ref[0] = log_a


def _proj(x, rope, w, *, tm, transposed, tq=256, tk=256, lead_rows=0):
    nb, s, d = x.shape
    assert s % tm == 0
    grid = (s // tm, nb)

    def tok(width, dtype):
        return (jax.ShapeDtypeStruct((nb, s, width), dtype),
                pl.BlockSpec((1, tm, width), lambda i, b: (b, i, 0)))

    def const(a):
        return pl.BlockSpec(a.shape, lambda i, b: (0,) * a.ndim)

    outs = []
    if transposed:
        assert tm % tq == 0 and tm % tk == 0
        outs.append((jax.ShapeDtypeStruct((nb, s // tq, QK_W_A, tq), BF16),
                     pl.BlockSpec((1, tm // tq, QK_W_A, tq), lambda i, b: (b, i, 0, 0))))
        outs.append(tok(QK_W_A, BF16))
        outs.append((jax.ShapeDtypeStruct((nb, s // tk, VAL_W_A, tk), BF16),
                     pl.BlockSpec((1, tm // tk, VAL_W_A, tk), lambda i, b: (b, i, 0, 0))))
    else:
        outs.append(tok(QK_W_A, BF16))
    kv_rows = tm * N_HEADS_A
    kv_shape = jax.ShapeDtypeStruct((nb, (lead_rows + s) * N_HEADS_A, HEAD_V_A), F32)
    if lead_rows:
        kv_spec = pl.BlockSpec((pl.Element(1), pl.Element(kv_rows), pl.Element(HEAD_V_A)),
                               lambda i, b: (b, pl.multiple_of(
                                   lead_rows * N_HEADS_A + i * kv_rows, 8), 0))
    else:
        kv_spec = pl.BlockSpec((1, kv_rows, HEAD_V_A), lambda i, b: (b, i, 0))
    outs += [(kv_shape, kv_spec), (kv_shape, kv_spec), tok(KEY_W_B, BF16), tok(KEY_W_B, BF16),
             tok(VAL_W_B, BF16), tok(VAL_W_B, BF16), tok(KEY_W_B, F32)]
    consts = [w['gmix'], w['w_main'], w['w_low'], w['w_a2'], w['b_a'], w['gq'], w['gk'], w['gsum']]
    return pl.pallas_call(
        functools.partial(_proj_kernel, transposed=transposed, tq=tq, tk=tk),
        grid=grid,
        in_specs=[pl.BlockSpec((1, tm, d), lambda i, b: (b, i, 0)),
                  pl.BlockSpec((tm, 3 * LANES), lambda i, b: (i, 0))] + [const(a) for a in consts],
        out_specs=[o[1] for o in outs],
        out_shape=[o[0] for o in outs],
        compiler_params=_cparams(2),
        name="proj_T" if transposed else "proj_small",
    )(x, rope, *consts)


def _colmax(s):
    rows = s.shape[0]
    slabs = 8 if rows % 64 == 0 else 1
    if slabs > 1:
        s = jnp.max(s.reshape(slabs, rows // slabs, s.shape[1]), axis=0)
    return jnp.max(s, axis=0, keepdims=True)


def _attn_kernel(lam_ref, qT_ref, k_ref, vT_ref, km_ref, vmT_ref, g_ref, o_ref,
                 qz_sc, s_sc, m_sc, acc_sc, *, tq, tk, cw, n_meta, hps):
    i = pl.program_id(2)
    n_cb = 2 * tq // cw
    chains = [(hh, cb) for hh in range(hps) for cb in range(n_cb)]

    def hrows(hh):
        return slice(hh * HEAD_V_A, (hh + 1) * HEAD_V_A)

    for hh in range(hps):
        qT = qT_ref[0, 0, hrows(hh), :]
        row = lax.broadcasted_iota(jnp.int32, qT.shape, 0)
        zero = jnp.zeros_like(qT)
        qz_sc[hh, :, :tq] = jnp.where(row < HEAD_DIM_A, qT, zero)
        qz_sc[hh, :, tq:] = jnp.where(row >= HEAD_DIM_A, qT, zero)

    def ext(vt):
        return jnp.concatenate([vt, jnp.ones((16, vt.shape[1]), BF16)], axis=0)

    def scores(j, hh, cb):
        kt = k_ref[0, pl.ds(pl.multiple_of(j * tk, tk), tk), hrows(hh)]
        return _dot(kt, qz_sc[hh, :, cb * cw:(cb + 1) * cw])

    def update(hh, cb, s, vt):
        cs = slice(cb * cw, (cb + 1) * cw)
        m_old = m_sc[hh, :, cs]
        m_new = jnp.maximum(m_old, _colmax(s))
        alpha = jnp.exp2(m_old - m_new)
        p = jnp.exp2(s - m_new).astype(BF16)
        m_sc[hh, :, cs] = m_new
        acc_sc[hh, :, cs] = alpha * acc_sc[hh, :, cs] + _dot(vt, p)

    s_meta = [_dot(km_ref[:, hrows(hh)], qz_sc[hh, :, cb * cw:(cb + 1) * cw]) for hh, cb in chains]
    for n, (hh, cb) in enumerate(chains):
        s_sc[n] = scores(0, hh, cb)
    for n, (hh, cb) in enumerate(chains):
        cs = slice(cb * cw, (cb + 1) * cw)
        s = s_meta[n]
        m0 = jnp.max(s, axis=0, keepdims=True)
        p = jnp.concatenate([jnp.exp2(s - m0).astype(BF16),
                             jnp.zeros((LANES - n_meta, cw), BF16)], axis=0)
        m_sc[hh, :, cs] = m0
        acc_sc[hh, :, cs] = _dot(ext(vmT_ref[hh]), p)

    def full_tile(j):
        for n, (hh, cb) in enumerate(chains):
            s = s_sc[n]
            s_sc[n] = scores(j + 1, hh, cb)
            update(hh, cb, s, ext(vT_ref[0, j, hrows(hh), :]))

    def body(jj, carry):
        full_tile(2 * jj)
        full_tile(2 * jj + 1)
        return carry

    lax.fori_loop(0, lax.shift_right_logical(i, 1), body, 0)

    @pl.when((i & 1) == 1)
    def _():
        full_tile(i - 1)

    for n, (hh, cb) in enumerate(chains):
        q0 = (cb * cw) % tq
        rows = min(tk, ((q0 + cw - 1) // CHUNK + 1) * CHUNK)
        s = s_sc[n, 0:rows, :]
        kr = lax.broadcasted_iota(jnp.int32, s.shape, 0) // CHUNK
        qc = (lax.broadcasted_iota(jnp.int32, s.shape, 1) + q0) // CHUNK
        update(hh, cb, jnp.where(kr <= qc, s, NEG), ext(vT_ref[0, i, hrows(hh), 0:rows]))

    lam = _diff_lambda(lam_ref)
    for hh in range(hps):
        o = acc_sc[hh, 0:HEAD_V_A, :] * (1.0 / acc_sc[hh, HEAD_V_A:HEAD_V_A + 1, :])
        oT = o[:, :tq] - lam * o[:, tq:]
        ms = jnp.mean(oT * oT, axis=0, keepdims=True)
        on = (oT * lax.rsqrt(ms + EPS)).T
        o_ref[0, :, hrows(hh)] = (on * g_ref[:, hrows(hh)] * (1.0 - LAMBDA_INIT)).astype(BF16)


def _attn(lam4, qT, k, vT, km, vmT, g_diff, *, n_meta):
    nb, nq, _, tq = qT.shape
    _, nk, _, tk = vT.shape
    s = k.shape[1]
    assert tq == tk and n_meta % 16 == 0
    cw = min(ATTN_COL_BLOCK, tq)
    hps = ATTN_HEADS_PER_STEP
    hw = hps * HEAD_V_A
    grid = (nb, N_HEADS_A // hps, nq)
    return pl.pallas_call(
        functools.partial(_attn_kernel, tq=tq, tk=tk, cw=cw, n_meta=n_meta, hps=hps),
        grid=grid,
        in_specs=[pl.BlockSpec(lam4.shape, lambda b, g, i: (0, 0)),
                  pl.BlockSpec((1, 1, hw, tq), lambda b, g, i: (b, i, g, 0)),
                  pl.BlockSpec((1, s, hw), lambda b, g, i: (b, 0, g)),
                  pl.BlockSpec((1, nk, hw, tk), lambda b, g, i: (b, 0, g, 0)),
                  pl.BlockSpec((n_meta, hw), lambda b, g, i: (0, g)),
                  pl.BlockSpec((hps, HEAD_V_A, LANES), lambda b, g, i: (g, 0, 0)),
                  pl.BlockSpec((1, hw), lambda b, g, i: (0, g))],
        out_specs=pl.BlockSpec((1, tq, hw), lambda b, g, i: (b, i, g)),
        out_shape=jax.ShapeDtypeStruct((nb, s, VAL_W_A), BF16),
        scratch_shapes=[pltpu.VMEM((hps, HEAD_V_A, 2 * tq), BF16),
                        pltpu.VMEM((hps * 2 * tq // cw, tk, cw), F32),
                        pltpu.VMEM((hps, 1, 2 * tq), F32),
                        pltpu.VMEM((hps, HEAD_V_A + 16, 2 * tq), F32)],
        compiler_params=_cparams(3),
        name="attn",
    )(lam4, qT, k, vT, km, vmT, g_diff)


def _sattn_kernel(lam_ref, q_ref, ck_ref, cv_ref, kn_ref, vn_ref, g_ref, o_ref, *, p_main, p_all):
    lam = _diff_lambda(lam_ref)
    lane = lax.broadcasted_iota(jnp.int32, (1, LANES), 1)
    for h in range(N_HEADS_A):
        hs = slice(h * HEAD_V_A, (h + 1) * HEAD_V_A)
        qh = q_ref[0, :, hs]
        t = qh.shape[0]

        def head_rows(ref, lo, n):
            return ref[0, pl.ds(lo * N_HEADS_A + h, n, stride=N_HEADS_A), :]

        ks = [head_rows(ck_ref, 0, p_main)]
        vs = [head_rows(cv_ref, 0, p_main)]
        if p_all > p_main:
            ks.append(jnp.concatenate([head_rows(ck_ref, p_main, p_all - p_main),
                                       head_rows(kn_ref, 0, t)], axis=0))
            vs.append(jnp.concatenate([head_rows(cv_ref, p_main, p_all - p_main),
                                       head_rows(vn_ref, 0, t)], axis=0))
        else:
            ks.append(head_rows(kn_ref, 0, t))
            vs.append(head_rows(vn_ref, 0, t))
        ks = [a.astype(BF16) for a in ks]
        vs = [a.astype(BF16) for a in vs]
        zero = jnp.zeros_like(qh)
        q2 = jnp.concatenate([jnp.where(lane < HEAD_DIM_A, qh, zero),
                              jnp.where(lane < HEAD_DIM_A, zero, qh)], axis=0)
        ss = [_dot_nt(q2, kk) for kk in ks]
        m = functools.reduce(jnp.maximum, [jnp.max(a, axis=-1, keepdims=True) for a in ss])
        ps = [jnp.exp2(a - m) for a in ss]
        l = sum(jnp.sum(a, axis=-1, keepdims=True) for a in ps)
        acc = sum(_dot(a.astype(BF16), vv) for a, vv in zip(ps, vs))
        o2 = acc * (1.0 / l)
        o = o2[:t] - lam * o2[t:]
        ms = jnp.mean(o * o, axis=-1, keepdims=True)
        y = o * lax.rsqrt(ms + EPS) * g_ref[:, hs] * (1.0 - LAMBDA_INIT)
        o_ref[0, :, hs] = y.astype(BF16)


def _sattn(lam4, q, ck, cv, kn, vn, g_diff):
    db, t, _ = q.shape
    p_all = ck.shape[1] // N_HEADS_A
    p_main = (p_all // LANES) * LANES
    assert (p_all - p_main + t) % 16 == 0

    def tokspec(n):
        return pl.BlockSpec((1, n, VAL_W_A), lambda b: (b, 0, 0))

    def rowspec(n):
        return pl.BlockSpec((1, n * N_HEADS_A, HEAD_V_A), lambda b: (b, 0, 0))

    return pl.pallas_call(
        functools.partial(_sattn_kernel, p_main=p_main, p_all=p_all),
        grid=(db,),
        in_specs=[pl.BlockSpec(lam4.shape, lambda b: (0, 0)), tokspec(t), rowspec(p_all),
                  rowspec(p_all), rowspec(t), rowspec(t),
                  pl.BlockSpec((1, VAL_W_A), lambda b: (0, 0))],
        out_specs=tokspec(t),
        out_shape=jax.ShapeDtypeStruct((db, t, VAL_W_A), BF16),
        compiler_params=_cparams(1),
        name="sattn",
    )(lam4, q, ck, cv, kn, vn, g_diff)


def _gla_kernel(q_ref, k_ref, v_ref, la_ref, sg_ref, s0_ref, g_ref, lbd_ref, y_ref, sout_ref,
                st_sc, o_sc, *, n_chunks):
    t = pl.program_id(1)
    n_pairs = N_HEADS_B // 2

    @pl.when(t == 0)
    def _():
        for p in range(n_pairs):
            st_sc[p] = s0_ref[0, p].T

    la = la_ref[0]
    hi, lo = _split_bf16(la)
    lbd = lbd_ref[...]
    wb = lbd.shape[0]
    b = jnp.concatenate([_dot(lbd, hi[r * wb:(r + 1) * wb]) + _dot(lbd, lo[r * wb:(r + 1) * wb])
                         for r in range(la.shape[0] // wb)], axis=0)
    q = q_ref[0].astype(F32)
    k = k_ref[0].astype(F32)
    qf = q * jnp.exp(b)
    kdec = (k * jnp.exp(-b)).astype(BF16)
    first = lax.broadcasted_iota(jnp.int32, (1, LANES), 1) < KEY_DIM_B
    tr = lax.broadcasted_iota(jnp.int32, (2 * CHUNK, CHUNK), 0) & (CHUNK - 1)
    tc = lax.broadcasted_iota(jnp.int32, (2 * CHUNK, CHUNK), 1)
    causal = tc <= tr
    blocks = [(c, p) for c in range(n_chunks) for p in range(n_pairs)]

    def rows_of(c):
        return slice(c * CHUNK, (c + 1) * CHUNK)

    def lanes_of(p):
        return slice(p * LANES, (p + 1) * LANES)

    def stack_heads(x):
        return jnp.concatenate([jnp.where(first, x, 0.0), jnp.where(first, 0.0, x)], axis=0)

    qm, sc = {}, {}
    for c, p in blocks:
        qm[c, p] = stack_heads(qf[rows_of(c), lanes_of(p)]).astype(BF16)
        sc[c, p] = _dot_nt(qm[c, p], kdec[rows_of(c), lanes_of(p)])
    ds, decay = {}, {}
    for c in range(n_chunks):
        b_last = b[(c + 1) * CHUNK - 1:(c + 1) * CHUNK, :]
        decay[c] = jnp.exp(b_last)
        kd2 = k[rows_of(c)] * jnp.exp(b_last - b[rows_of(c)])
        for p in range(n_pairs):
            v2 = jnp.concatenate([v_ref[0, rows_of(c), (2 * p + a) * VAL_DIM_B:(2 * p + a + 1) * VAL_DIM_B]
                                  for a in range(2)], axis=0)
            ds[c, p] = _dot_tn(v2, stack_heads(kd2[:, lanes_of(p)]).astype(BF16))
    st_in = {}
    for p in range(n_pairs):
        st = st_sc[p]
        for c in range(n_chunks):
            st_in[c, p] = st.astype(BF16)
            st = st * decay[c][:, lanes_of(p)] + ds[c, p]
        st_sc[p] = st
    for c, p in blocks:
        o_inter = _dot_nt(qm[c, p], st_in[c, p])
        s_cp = jnp.where(causal, sc[c, p], 0.0).astype(BF16)
        o_intra = _dot(s_cp, v_ref[0, rows_of(c), 2 * p * VAL_DIM_B:(2 * p + 2) * VAL_DIM_B])
        for a in range(2):
            h = 2 * p + a
            o_sc[rows_of(c), h * VAL_DIM_B:(h + 1) * VAL_DIM_B] = (
                o_inter[a * CHUNK:(a + 1) * CHUNK]
                + o_intra[a * CHUNK:(a + 1) * CHUNK, a * VAL_DIM_B:(a + 1) * VAL_DIM_B])

    for h in range(N_HEADS_B):
        hs = slice(h * VAL_DIM_B, (h + 1) * VAL_DIM_B)
        o = o_sc[:, hs]
        ms = jnp.mean(o * o, axis=-1, keepdims=True)
        y = o * lax.rsqrt(ms + EPS) * g_ref[:, hs]
        y_ref[0, :, hs] = (y * sg_ref[0, :, hs].astype(F32)).astype(BF16)

    @pl.when(t == pl.num_programs(1) - 1)
    def _():
        for p in range(n_pairs):
            sout_ref[0, p] = st_sc[p].T


def _gla(q, k, v, la, sg, s0, g_gla, *, tg):
    nb, s, _ = q.shape
    assert s % tg == 0 and tg % CHUNK == 0
    n_chunks = tg // CHUNK
    s0p = s0.reshape(s0.shape[0], N_HEADS_B // 2, 2 * KEY_DIM_B, VAL_DIM_B)
    shared = s0p.shape[0] == 1
    wb = min(tg, MXU_WIDTH)
    idx = jnp.arange(wb)
    lbd = ((idx[:, None] // CHUNK == idx[None, :] // CHUNK)
           & (idx[None, :] <= idx[:, None])).astype(BF16)

    def tok(width):
        return pl.BlockSpec((1, tg, width), lambda b, t: (b, t, 0))

    sspec_in = pl.BlockSpec((1,) + s0p.shape[1:], (lambda b, t: (0, 0, 0, 0)) if shared
                            else (lambda b, t: (b, 0, 0, 0)))
    y, s_out = pl.pallas_call(
        functools.partial(_gla_kernel, n_chunks=n_chunks),
        grid=(nb, s // tg),
        in_specs=[tok(KEY_W_B), tok(KEY_W_B), tok(VAL_W_B), tok(KEY_W_B), tok(VAL_W_B), sspec_in,
                  pl.BlockSpec((1, VAL_W_B), lambda b, t: (0, 0)),
                  pl.BlockSpec((wb, wb), lambda b, t: (0, 0))],
        out_specs=[tok(VAL_W_B), pl.BlockSpec((1,) + s0p.shape[1:], lambda b, t: (b, 0, 0, 0))],
        out_shape=[jax.ShapeDtypeStruct((nb, s, VAL_W_B), BF16),
                   jax.ShapeDtypeStruct((nb,) + s0p.shape[1:], F32)],
        scratch_shapes=[pltpu.VMEM((N_HEADS_B // 2, VAL_DIM_B, 2 * KEY_DIM_B), F32),
                        pltpu.VMEM((tg, VAL_W_B), F32)],
        compiler_params=_cparams(2),
        name="gla",
    )(q, k, v, la, sg, s0p, g_gla, lbd)
    return y, s_out.reshape(nb, N_HEADS_B, KEY_DIM_B, VAL_DIM_B)


def _ffn_kernel(x_ref, ya_ref, yb_ref, woa_ref, wob_ref, gn_ref, wg_ref, wu_ref, wd_ref, o_ref,
                *, n_split):
    h = x_ref[...] + _dot(ya_ref[...], woa_ref[...]) + _dot(yb_ref[...], wob_ref[...])
    ms = jnp.mean(h * h, axis=-1, keepdims=True)
    hn = (h * lax.rsqrt(ms + EPS) * gn_ref[...]).astype(BF16)
    tiles = D_FF // MXU_WIDTH
    bounds = [MXU_WIDTH * ((tiles * c + n_split - 1) // n_split) for c in range(n_split + 1)]
    f = None
    for c in range(n_split):
        cs = slice(bounds[c], bounds[c + 1])
        g = _dot(hn, wg_ref[:, cs])
        u = _dot(hn, wu_ref[:, cs])
        a = (g * (1.0 / (1.0 + jnp.exp(-g))) * u).astype(BF16)
        d = _dot(a, wd_ref[cs, :])
        f = d if f is None else f + d
    o_ref[...] = h + f


def _ffn(x, ya, yb, w, *, tm, n_split=2):
    t, d = x.shape
    assert t % tm == 0

    def const(a):
        return pl.BlockSpec(a.shape, lambda i: (0,) * a.ndim)

    consts = [w['w_out_a'], w['w_out_b'], w['g_ffn'], w['w_gate'], w['w_up'], w['w_down']]
    return pl.pallas_call(
        functools.partial(_ffn_kernel, n_split=n_split),
        grid=(t // tm,),
        in_specs=[pl.BlockSpec((tm, d), lambda i: (i, 0)),
                  pl.BlockSpec((tm, VAL_W_A), lambda i: (i, 0)),
                  pl.BlockSpec((tm, VAL_W_B), lambda i: (i, 0))] + [const(a) for a in consts],
        out_specs=pl.BlockSpec((tm, d), lambda i: (i, 0)),
        out_shape=jax.ShapeDtypeStruct((t, d), F32),
        compiler_params=_cparams(1),
        name="ffn",
    )(x, ya, yb, *consts)


def _put_lead_kernel(k_any, v_any, km_ref, vm_ref, ko_ref, vo_ref):
    del k_any, v_any
    ko_ref[0] = km_ref[0]
    vo_ref[0] = vm_ref[0]


def _put_lead_rows(k, v, k_lead, v_lead):
    nb = k.shape[0]
    n_lead = k_lead.shape[1]
    lead = pl.BlockSpec((1, n_lead, HEAD_V_A), lambda b: (0, 0, 0))
    out = pl.BlockSpec((1, n_lead, HEAD_V_A), lambda b: (b, 0, 0))
    return pl.pallas_call(
        _put_lead_kernel,
        grid=(nb,),
        in_specs=[pl.BlockSpec(memory_space=pl.ANY), pl.BlockSpec(memory_space=pl.ANY), lead, lead],
        out_specs=[out, out],
        out_shape=[jax.ShapeDtypeStruct(k.shape, k.dtype), jax.ShapeDtypeStruct(v.shape, v.dtype)],
        input_output_aliases={0: 0, 1: 1},
        compiler_params=_cparams(1),
        name="put_lead",
    )(k, v, k_lead, v_lead)


def _rope_table(pos):
    half = ROT_DIM // 2
    inv_freq = ROPE_THETA ** (-jnp.arange(0, ROT_DIM, 2, dtype=F32) / ROT_DIM)
    ang = pos.astype(F32)[:, None] * inv_freq[None, :]
    cos, sin = jnp.cos(ang), jnp.sin(ang)
    lane = jnp.arange(LANES) % HEAD_DIM_A
    j = lane % half
    c = jnp.where(lane[None, :] < ROT_DIM, cos[:, j], 1.0)
    sa = jnp.where(lane[None, :] < half, -sin[:, j], 0.0)
    sb = jnp.where((lane[None, :] >= half) & (lane[None, :] < ROT_DIM), sin[:, j], 0.0)
    return jnp.concatenate([c, sa, sb], axis=1).astype(F32)


def kernel(x_prompt, x_sample, cache_k_diff, cache_v_diff, state_gla, meta_tokens, norm_mix, w_in,
           w_a2, b_a, q_norm, k_norm, lambda_q1, lambda_k1, lambda_q2, lambda_k2, g_diff, g_gla,
           w_out, norm_ffn, w_ffn_gate, w_ffn_up, w_ffn_down):
    depth = w_in.shape[0]
    assert depth == 1, "single-layer step"
    bsz, seq, d = x_prompt.shape
    db, dt, _ = x_sample.shape
    n_meta = meta_tokens.shape[0]
    past = cache_k_diff.shape[2]
    assert d == D_MODEL and n_meta <= LANES and seq % CHUNK == 0

    w_in0 = w_in[0]
    gid = jnp.arange(MXU_WIDTH) // HEAD_DIM_A
    w = {
        'gmix': norm_mix[0][None, :],
        'w_main': w_in0[:, :N_MAIN].astype(BF16),
        'w_low': jnp.pad(w_in0[:, N_MAIN:], ((0, 0), (0, LANES - GATE_RANK))).astype(BF16),
        'w_a2': jnp.pad(w_a2[0], ((0, LANES - GATE_RANK), (0, 0))).astype(BF16),
        'b_a': b_a[0][None, :],
        'gq': jnp.tile(q_norm[0], QK_W_A // HEAD_DIM_A)[None, :],
        'gk': jnp.tile(k_norm[0], QK_W_A // HEAD_DIM_A)[None, :],
        'gsum': (gid[:, None] == gid[None, :]).astype(BF16),
        'w_out_a': w_out[0][:VAL_W_A].astype(BF16),
        'w_out_b': w_out[0][VAL_W_A:].astype(BF16),
        'g_ffn': norm_ffn[0][None, :],
        'w_gate': w_ffn_gate[0].astype(BF16),
        'w_up': w_ffn_up[0].astype(BF16),
        'w_down': w_ffn_down[0].astype(BF16),
    }
    lam4 = jnp.stack([lambda_q1[0], lambda_k1[0], lambda_q2[0], lambda_k2[0]])
    g_diff2 = g_diff[0][None, :]
    g_gla2 = g_gla[0][None, :]

    rope_m = _rope_table(jnp.arange(n_meta, dtype=jnp.int32))
    (_, kf_m, vf_m, qb_m, kb_m, vb_m, sg_m, la_m) = _proj(
        meta_tokens[None], rope_m, w, tm=n_meta, transposed=False)
    pad_m = ((0, 0), (0, CHUNK - n_meta), (0, 0))
    s_zero = jnp.zeros((1, N_HEADS_B, KEY_DIM_B, VAL_DIM_B), F32)
    _, s_meta = _gla(jnp.pad(qb_m, pad_m), jnp.pad(kb_m, pad_m), jnp.pad(vb_m, pad_m),
                     jnp.pad(la_m, pad_m), jnp.pad(sg_m, pad_m), s_zero, g_gla2, tg=CHUNK)

    tm = 512 if seq % 512 == 0 else seq
    tqk = 512 if seq % 512 == 0 else seq
    tg = 2 * tm if seq % (2 * tm) == 0 else tm
    rope_x = _rope_table(n_meta + jnp.arange(seq, dtype=jnp.int32))
    (qT, k_bf, vT, kf_x, vf_x, qb_x, kb_x, vb_x, sg_x, la_x) = _proj(
        x_prompt, rope_x, w, tm=tg, transposed=True, tq=tqk, tk=tqk, lead_rows=n_meta)
    km = kf_m.reshape(n_meta, QK_W_A).astype(BF16)
    vmT = jnp.pad(vf_m.reshape(n_meta, N_HEADS_A, HEAD_V_A).transpose(1, 2, 0),
                  ((0, 0), (0, 0), (0, LANES - n_meta))).astype(BF16)
    ya_x = _attn(lam4, qT, k_bf, vT, km, vmT, g_diff2, n_meta=n_meta)
    yb_x, s_final = _gla(qb_x, kb_x, vb_x, la_x, sg_x, s_meta, g_gla2, tg=tg)
    y_prompt = _ffn(x_prompt.reshape(bsz * seq, d), ya_x.reshape(bsz * seq, VAL_W_A),
                    yb_x.reshape(bsz * seq, VAL_W_B), w, tm=tm).reshape(bsz, seq, d)
    new_k_p, new_v_p = _put_lead_rows(kf_x, vf_x, kf_m, vf_m)

    ts = db * dt
    rope_s = jnp.tile(_rope_table(past + jnp.arange(dt, dtype=jnp.int32)), (db, 1))
    (q_s, kf_s, vf_s, qb_s, kb_s, vb_s, sg_s, la_s) = _proj(
        x_sample.reshape(1, ts, d), rope_s, w, tm=ts, transposed=False)
    ya_s = _sattn(lam4, q_s.reshape(db, dt, QK_W_A),
                  cache_k_diff.reshape(db, past * N_HEADS_A, HEAD_V_A),
                  cache_v_diff.reshape(db, past * N_HEADS_A, HEAD_V_A),
                  kf_s.reshape(db, dt * N_HEADS_A, HEAD_V_A),
                  vf_s.reshape(db, dt * N_HEADS_A, HEAD_V_A), g_diff2)
    pad_s = ((0, 0), (0, CHUNK - dt), (0, 0))

    def stream(a):
        return jnp.pad(a.reshape(db, dt, a.shape[-1]), pad_s)

    yb_s, s_new = _gla(stream(qb_s), stream(kb_s), stream(vb_s), stream(la_s), stream(sg_s),
                       state_gla[0], g_gla2, tg=CHUNK)
    y_sample = _ffn(x_sample.reshape(ts, d), ya_s.reshape(ts, VAL_W_A),
                    yb_s[:, :dt].reshape(ts, VAL_W_B), w, tm=ts).reshape(db, dt, d)

    return (y_prompt, y_sample,
            new_k_p.reshape(1, bsz, n_meta + seq, N_HEADS_A, 2 * HEAD_DIM_A),
            new_v_p.reshape(1, bsz, n_meta + seq, N_HEADS_A, HEAD_V_A),
            s_final[None],
            kf_s.reshape(1, db, dt, N_HEADS_A, 2 * HEAD_DIM_A),
            vf_s.reshape(1, db, dt, N_HEADS_A, HEAD_V_A),
            s_new[None])
```

```python
import functools
import math

import jax
import jax.numpy as jnp
from jax import lax
from jax.experimental import pallas as pl
from jax.experimental.pallas import tpu as pltpu

F32 = jnp.float32
BF16 = jnp.bfloat16

D_MODEL = 1024
N_HEADS_A = 4
HEAD_DIM_A = 64
HEAD_V_A = 128
QK_W_A = 512
VAL_W_A = 512
ROT_DIM = 16
ROPE_THETA = 500000.0
N_HEADS_B = 4
KEY_DIM_B = 64
VAL_DIM_B = 128
KEY_W_B = 256
VAL_W_B = 512
GATE_RANK = 16
GATE_TAU = 16.0
D_FF = 2816
EPS = 1e-6
CHUNK = 64
LAMBDA_INIT = 0.8 - 0.6 * math.exp(-0.3 * 0)
N_MAIN = 2 * QK_W_A + VAL_W_A + 2 * KEY_W_B + 2 * VAL_W_B
LANES = 128
NEG = -1e30
LOG2E = 1.4426950408889634
MXU_WIDTH = 256
ATTN_COL_BLOCK = MXU_WIDTH
ATTN_HEADS_PER_STEP = 4
ATTN_TILES_PER_TRIP = 4
ATTN_LOOKAHEAD = 4

VMEM_LIMIT = 56 * 1024 * 1024


def _cparams(n_axes):
    return pltpu.CompilerParams(dimension_semantics=("arbitrary",) * n_axes,
                                vmem_limit_bytes=VMEM_LIMIT)


def _dot(a, b):
    return jnp.dot(a, b, preferred_element_type=F32)


def _dot_nt(a, b):
    return lax.dot_general(a, b, (((1,), (1,)), ((), ())), preferred_element_type=F32)


def _dot_tn(a, b):
    return lax.dot_general(a, b, (((0,), (0,)), ((), ())), preferred_element_type=F32)


def _split_bf16(x):
    hi = x.astype(BF16)
    lo = (x - hi.astype(F32)).astype(BF16)
    return hi, lo


def _diff_lambda(lam_ref):
    l4 = lam_ref[...]
    s1 = jnp.sum(l4[0:1] * l4[1:2], axis=-1, keepdims=True)
    s2 = jnp.sum(l4[2:3] * l4[3:4], axis=-1, keepdims=True)
    return jnp.exp(s1) - jnp.exp(s2) + LAMBDA_INIT


def _proj_kernel(x_ref, rope_ref, gmix_ref, wmain_ref, wlow_ref, wa2_ref, ba_ref, gq_ref, gk_ref,
                 gsum_ref, *out_refs, transposed, tq, tk):
    x = x_ref[0]
    tm = x.shape[0]
    ms = jnp.mean(x * x, axis=-1, keepdims=True)
    hn = (x * lax.rsqrt(ms + EPS) * gmix_ref[...]).astype(BF16)

    def seg(lo, hi):
        return _dot(hn, wmain_ref[:, lo:hi])

    cos = rope_ref[:, 0:LANES]
    sin_a = rope_ref[:, LANES:2 * LANES]
    sin_b = rope_ref[:, 2 * LANES:3 * LANES]
    gsum = gsum_ref[...]

    def qknorm_rope(p, g_ref):
        p2 = (p * p).astype(BF16)
        wb = gsum.shape[0]
        ss = jnp.concatenate([_dot(p2[:, c * wb:(c + 1) * wb], gsum)
                              for c in range(QK_W_A // wb)], axis=1)
        pn = p * lax.rsqrt(ss * (1.0 / HEAD_DIM_A) + EPS) * g_ref[...]
        cols = []
        for c in range(QK_W_A // LANES):
            blk = pn[:, c * LANES:(c + 1) * LANES]
            cols.append(blk * cos + pltpu.roll(blk, LANES - ROT_DIM // 2, 1) * sin_a
                        + pltpu.roll(blk, ROT_DIM // 2, 1) * sin_b)
        return jnp.concatenate(cols, axis=1)

    o_qa, o_ka, o_va = 0, QK_W_A, 2 * QK_W_A
    o_qb = o_va + VAL_W_A
    o_kb, o_vb = o_qb + KEY_W_B, o_qb + 2 * KEY_W_B
    o_gb = o_vb + VAL_W_B
    qa = qknorm_rope(seg(o_qa, o_qa + QK_W_A), gq_ref)
    qa = qa * (HEAD_DIM_A ** -0.5 * LOG2E)
    ka = qknorm_rope(seg(o_ka, o_ka + QK_W_A), gk_ref)
    va = seg(o_va, o_va + VAL_W_A)
    qb = seg(o_qb, o_qb + KEY_W_B) * (KEY_DIM_B ** -0.5)
    kb = seg(o_kb, o_kb + KEY_W_B)
    vb = seg(o_vb, o_vb + VAL_W_B)
    gb = seg(o_gb, o_gb + VAL_W_B)
    sg = gb * (1.0 / (1.0 + jnp.exp(-gb)))
    a_low = _dot(hn, wlow_ref[...])
    gate = _dot(a_low.astype(BF16), wa2_ref[...]) + ba_ref[...]
    log_a = (jnp.minimum(gate, 0.0) - jnp.log(1.0 + jnp.exp(-jnp.abs(gate)))) * (1.0 / GATE_TAU)

    if transposed:
        qT_ref, k_ref, vT_ref, kf_ref, vf_ref, qb_ref, kb_ref, vb_ref, sg_ref, la_ref = out_refs
        for r in range(tm // tq):
            qT_ref[0, r] = qa[r * tq:(r + 1) * tq, :].T.astype(BF16)
        for r in range(tm // tk):
            vT_ref[0, r] = va[r * tk:(r + 1) * tk, :].T.astype(BF16)
        k_ref[0] = ka.astype(BF16)
    else:
        q_ref, kf_ref, vf_ref, qb_ref, kb_ref, vb_ref, sg_ref, la_ref = out_refs
        q_ref[0] = qa.astype(BF16)
    for h in range(N_HEADS_A):
        hs = slice(h * HEAD_V_A, (h + 1) * HEAD_V_A)
        kf_ref[0, pl.ds(h, tm, stride=N_HEADS_A), :] = ka[:, hs]
        vf_ref[0, pl.ds(h, tm, stride=N_HEADS_A), :] = va[:, hs]
    qb_ref[0] = qb.astype(BF16)
    kb_ref[0] = kb.astype(BF16)
    vb_ref[0] = vb.astype(BF16)
    sg_ref[0] = sg.astype(BF16)
    la_ref[0] = log_a


def _proj(x, rope, w, *, tm, transposed, tq=256, tk=256, lead_rows=0):
    nb, s, d = x.shape
    assert s % tm == 0
    grid = (s // tm, nb)

    def tok(width, dtype):
        return (jax.ShapeDtypeStruct((nb, s, width), dtype),
                pl.BlockSpec((1, tm, width), lambda i, b: (b, i, 0)))

    def const(a):
        return pl.BlockSpec(a.shape, lambda i, b: (0,) * a.ndim)

    outs = []
    if transposed:
        assert tm % tq == 0 and tm % tk == 0
        outs.append((jax.ShapeDtypeStruct((nb, s // tq, QK_W_A, tq), BF16),
                     pl.BlockSpec((1, tm // tq, QK_W_A, tq), lambda i, b: (b, i, 0, 0))))
        outs.append(tok(QK_W_A, BF16))
        outs.append((jax.ShapeDtypeStruct((nb, s // tk, VAL_W_A, tk), BF16),
                     pl.BlockSpec((1, tm // tk, VAL_W_A, tk), lambda i, b: (b, i, 0, 0))))
    else:
        outs.append(tok(QK_W_A, BF16))
    kv_rows = tm * N_HEADS_A
    kv_shape = jax.ShapeDtypeStruct((nb, (lead_rows + s) * N_HEADS_A, HEAD_V_A), F32)
    if lead_rows:
        kv_spec = pl.BlockSpec((pl.Element(1), pl.Element(kv_rows), pl.Element(HEAD_V_A)),
                               lambda i, b: (b, pl.multiple_of(
                                   lead_rows * N_HEADS_A + i * kv_rows, 8), 0))
    else:
        kv_spec = pl.BlockSpec((1, kv_rows, HEAD_V_A), lambda i, b: (b, i, 0))
    outs += [(kv_shape, kv_spec), (kv_shape, kv_spec), tok(KEY_W_B, BF16), tok(KEY_W_B, BF16),
             tok(VAL_W_B, BF16), tok(VAL_W_B, BF16), tok(KEY_W_B, F32)]
    consts = [w['gmix'], w['w_main'], w['w_low'], w['w_a2'], w['b_a'], w['gq'], w['gk'], w['gsum']]
    return pl.pallas_call(
        functools.partial(_proj_kernel, transposed=transposed, tq=tq, tk=tk),
        grid=grid,
        in_specs=[pl.BlockSpec((1, tm, d), lambda i, b: (b, i, 0)),
                  pl.BlockSpec((tm, 3 * LANES), lambda i, b: (i, 0))] + [const(a) for a in consts],
        out_specs=[o[1] for o in outs],
        out_shape=[o[0] for o in outs],
        compiler_params=_cparams(2),
        name="proj_T" if transposed else "proj_small",
    )(x, rope, *consts)


def _colmax(s):
    rows = s.shape[0]
    slabs = 8 if rows % 64 == 0 else 1
    if slabs > 1:
        s = jnp.max(s.reshape(slabs, rows // slabs, s.shape[1]), axis=0)
    return jnp.max(s, axis=0, keepdims=True)


def _attn_kernel(lam_ref, qT_ref, k_ref, vT_ref, km_ref, vmT_ref, g_ref, o_ref,
                 qz_sc, s_sc, m_sc, acc_sc, *, tq, tk, cw, n_meta, hps):
    i = pl.program_id(2)
    n_cb = 2 * tq // cw
    chains = [(hh, cb) for hh in range(hps) for cb in range(n_cb)]

    def hrows(hh):
        return slice(hh * HEAD_V_A, (hh + 1) * HEAD_V_A)

    for hh in range(hps):
        qT = qT_ref[0, 0, hrows(hh), :]
        row = lax.broadcasted_iota(jnp.int32, qT.shape, 0)
        zero = jnp.zeros_like(qT)
        qz_sc[hh, :, :tq] = jnp.where(row < HEAD_DIM_A, qT, zero)
        qz_sc[hh, :, tq:] = jnp.where(row >= HEAD_DIM_A, qT, zero)

    def ext(vt):
        return jnp.concatenate([vt, jnp.ones((16, vt.shape[1]), BF16)], axis=0)

    def scores(j, hh, cb):
        kt = k_ref[0, pl.ds(pl.multiple_of(j * tk, tk), tk), hrows(hh)]
        return _dot(kt, qz_sc[hh, :, cb * cw:(cb + 1) * cw])

    def update(hh, cb, s, vt):
        cs = slice(cb * cw, (cb + 1) * cw)
        m_old = m_sc[hh, :, cs]
        m_new = jnp.maximum(m_old, _colmax(s))
        alpha = jnp.exp2(m_old - m_new)
        p = jnp.exp2(s - m_new).astype(BF16)
        m_sc[hh, :, cs] = m_new
        acc_sc[hh, :, cs] = alpha * acc_sc[hh, :, cs] + _dot(vt, p)

    look = ATTN_LOOKAHEAD
    n_ch = len(chains)
    assert n_ch % look == 0

    def prefetch(j, n):
        nn = n + look
        hh, cb = chains[nn % n_ch]
        s_sc[n % look] = scores(j + nn // n_ch, hh, cb)

    s_meta = [_dot(km_ref[:, hrows(hh)], qz_sc[hh, :, cb * cw:(cb + 1) * cw]) for hh, cb in chains]
    for n in range(look):
        s_sc[n] = scores(0, *chains[n])
    for n, (hh, cb) in enumerate(chains):
        cs = slice(cb * cw, (cb + 1) * cw)
        s = s_meta[n]
        m0 = jnp.max(s, axis=0, keepdims=True)
        p = jnp.concatenate([jnp.exp2(s - m0).astype(BF16),
                             jnp.zeros((LANES - n_meta, cw), BF16)], axis=0)
        m_sc[hh, :, cs] = m0
        acc_sc[hh, :, cs] = _dot(ext(vmT_ref[hh]), p)

    def full_tile(j):
        for n, (hh, cb) in enumerate(chains):
            s = s_sc[n % look]
            prefetch(j, n)
            update(hh, cb, s, ext(vT_ref[0, j, hrows(hh), :]))

    def body(jj, carry):
        for u in range(ATTN_TILES_PER_TRIP):
            full_tile(ATTN_TILES_PER_TRIP * jj + u)
        return carry

    shift = ATTN_TILES_PER_TRIP.bit_length() - 1
    lax.fori_loop(0, lax.shift_right_logical(i, shift), body, 0)
    done = lax.shift_left(lax.shift_right_logical(i, shift), shift)
    bit = ATTN_TILES_PER_TRIP // 2
    while bit:
        @pl.when((i & bit) != 0)
        def _(done=done, bit=bit):
            for u in range(bit):
                full_tile(done + u)
        done = done + (i & bit)
        bit //= 2

    for n, (hh, cb) in enumerate(chains):
        q0 = (cb * cw) % tq
        rows = min(tk, ((q0 + cw - 1) // CHUNK + 1) * CHUNK)
        s = s_sc[n % look, 0:rows, :]
        if n + look < n_ch:
            prefetch(i, n)
        kr = lax.broadcasted_iota(jnp.int32, s.shape, 0) // CHUNK
        qc = (lax.broadcasted_iota(jnp.int32, s.shape, 1) + q0) // CHUNK
        update(hh, cb, jnp.where(kr <= qc, s, NEG), ext(vT_ref[0, i, hrows(hh), 0:rows]))

    lam = _diff_lambda(lam_ref)
    for hh in range(hps):
        o = acc_sc[hh, 0:HEAD_V_A, :] * (1.0 / acc_sc[hh, HEAD_V_A:HEAD_V_A + 1, :])
        oT = o[:, :tq] - lam * o[:, tq:]
        ms = jnp.mean(oT * oT, axis=0, keepdims=True)
        on = (oT * lax.rsqrt(ms + EPS)).T
        o_ref[0, :, hrows(hh)] = (on * g_ref[:, hrows(hh)] * (1.0 - LAMBDA_INIT)).astype(BF16)


def _attn(lam4, qT, k, vT, km, vmT, g_diff, *, n_meta):
    nb, nq, _, tq = qT.shape
    _, nk, _, tk = vT.shape
    s = k.shape[1]
    assert tq == tk and n_meta % 16 == 0
    cw = min(ATTN_COL_BLOCK, tq)
    hps = ATTN_HEADS_PER_STEP
    hw = hps * HEAD_V_A
    grid = (nb, N_HEADS_A // hps, nq)
    return pl.pallas_call(
        functools.partial(_attn_kernel, tq=tq, tk=tk, cw=cw, n_meta=n_meta, hps=hps),
        grid=grid,
        in_specs=[pl.BlockSpec(lam4.shape, lambda b, g, i: (0, 0)),
                  pl.BlockSpec((1, 1, hw, tq), lambda b, g, i: (b, i, g, 0)),
                  pl.BlockSpec((1, s, hw), lambda b, g, i: (b, 0, g)),
                  pl.BlockSpec((1, nk, hw, tk), lambda b, g, i: (b, 0, g, 0)),
                  pl.BlockSpec((n_meta, hw), lambda b, g, i: (0, g)),
                  pl.BlockSpec((hps, HEAD_V_A, LANES), lambda b, g, i: (g, 0, 0)),
                  pl.BlockSpec((1, hw), lambda b, g, i: (0, g))],
        out_specs=pl.BlockSpec((1, tq, hw), lambda b, g, i: (b, i, g)),
        out_shape=jax.ShapeDtypeStruct((nb, s, VAL_W_A), BF16),
        scratch_shapes=[pltpu.VMEM((hps, HEAD_V_A, 2 * tq), BF16),
                        pltpu.VMEM((ATTN_LOOKAHEAD, tk, cw), F32),
                        pltpu.VMEM((hps, 1, 2 * tq), F32),
                        pltpu.VMEM((hps, HEAD_V_A + 16, 2 * tq), F32)],
        compiler_params=_cparams(3),
        name="attn",
    )(lam4, qT, k, vT, km, vmT, g_diff)


def _sattn_kernel(lam_ref, q_ref, ck_ref, cv_ref, kn_ref, vn_ref, g_ref, o_ref, *, p_main, p_all):
    lam = _diff_lambda(lam_ref)
    lane = lax.broadcasted_iota(jnp.int32, (1, LANES), 1)
    for h in range(N_HEADS_A):
        hs = slice(h * HEAD_V_A, (h + 1) * HEAD_V_A)
        qh = q_ref[0, :, hs]
        t = qh.shape[0]

        def head_rows(ref, lo, n):
            return ref[0, pl.ds(lo * N_HEADS_A + h, n, stride=N_HEADS_A), :]

        ks = [head_rows(ck_ref, 0, p_main)]
        vs = [head_rows(cv_ref, 0, p_main)]
        if p_all > p_main:
            ks.append(jnp.concatenate([head_rows(ck_ref, p_main, p_all - p_main),
                                       head_rows(kn_ref, 0, t)], axis=0))
            vs.append(jnp.concatenate([head_rows(cv_ref, p_main, p_all - p_main),
                                       head_rows(vn_ref, 0, t)], axis=0))
        else:
            ks.append(head_rows(kn_ref, 0, t))
            vs.append(head_rows(vn_ref, 0, t))
        ks = [a.astype(BF16) for a in ks]
        vs = [a.astype(BF16) for a in vs]
        zero = jnp.zeros_like(qh)
        q2 = jnp.concatenate([jnp.where(lane < HEAD_DIM_A, qh, zero),
                              jnp.where(lane < HEAD_DIM_A, zero, qh)], axis=0)
        ss = [_dot_nt(q2, kk) for kk in ks]
        m = functools.reduce(jnp.maximum, [jnp.max(a, axis=-1, keepdims=True) for a in ss])
        ps = [jnp.exp2(a - m) for a in ss]
        l = sum(jnp.sum(a, axis=-1, keepdims=True) for a in ps)
        acc = sum(_dot(a.astype(BF16), vv) for a, vv in zip(ps, vs))
        o2 = acc * (1.0 / l)
        o = o2[:t] - lam * o2[t:]
        ms = jnp.mean(o * o, axis=-1, keepdims=True)
        y = o * lax.rsqrt(ms + EPS) * g_ref[:, hs] * (1.0 - LAMBDA_INIT)
        o_ref[0, :, hs] = y.astype(BF16)


def _sattn(lam4, q, ck, cv, kn, vn, g_diff):
    db, t, _ = q.shape
    p_all = ck.shape[1] // N_HEADS_A
    p_main = (p_all // LANES) * LANES
    assert (p_all - p_main + t) % 16 == 0

    def tokspec(n):
        return pl.BlockSpec((1, n, VAL_W_A), lambda b: (b, 0, 0))

    def rowspec(n):
        return pl.BlockSpec((1, n * N_HEADS_A, HEAD_V_A), lambda b: (b, 0, 0))

    return pl.pallas_call(
        functools.partial(_sattn_kernel, p_main=p_main, p_all=p_all),
        grid=(db,),
        in_specs=[pl.BlockSpec(lam4.shape, lambda b: (0, 0)), tokspec(t), rowspec(p_all),
                  rowspec(p_all), rowspec(t), rowspec(t),
                  pl.BlockSpec((1, VAL_W_A), lambda b: (0, 0))],
        out_specs=tokspec(t),
        out_shape=jax.ShapeDtypeStruct((db, t, VAL_W_A), BF16),
        compiler_params=_cparams(1),
        name="sattn",
    )(lam4, q, ck, cv, kn, vn, g_diff)


def _gla_kernel(q_ref, k_ref, v_ref, la_ref, sg_ref, s0_ref, g_ref, lbd_ref, y_ref, sout_ref,
                st_sc, o_sc, *, n_chunks):
    t = pl.program_id(1)
    n_pairs = N_HEADS_B // 2

    @pl.when(t == 0)
    def _():
        for p in range(n_pairs):
            st_sc[p] = s0_ref[0, p].T

    la = la_ref[0]
    hi, lo = _split_bf16(la)
    lbd = lbd_ref[...]
    wb = lbd.shape[0]
    b = jnp.concatenate([_dot(lbd, hi[r * wb:(r + 1) * wb]) + _dot(lbd, lo[r * wb:(r + 1) * wb])
                         for r in range(la.shape[0] // wb)], axis=0)
    q = q_ref[0].astype(F32)
    k = k_ref[0].astype(F32)
    qf = q * jnp.exp(b)
    kdec = (k * jnp.exp(-b)).astype(BF16)
    first = lax.broadcasted_iota(jnp.int32, (1, LANES), 1) < KEY_DIM_B
    tr = lax.broadcasted_iota(jnp.int32, (2 * CHUNK, CHUNK), 0) & (CHUNK - 1)
    tc = lax.broadcasted_iota(jnp.int32, (2 * CHUNK, CHUNK), 1)
    causal = tc <= tr
    blocks = [(c, p) for c in range(n_chunks) for p in range(n_pairs)]

    def rows_of(c):
        return slice(c * CHUNK, (c + 1) * CHUNK)

    def lanes_of(p):
        return slice(p * LANES, (p + 1) * LANES)

    def stack_heads(x):
        return jnp.concatenate([jnp.where(first, x, 0.0), jnp.where(first, 0.0, x)], axis=0)

    qm, sc = {}, {}
    for c, p in blocks:
        qm[c, p] = stack_heads(qf[rows_of(c), lanes_of(p)]).astype(BF16)
        sc[c, p] = _dot_nt(qm[c, p], kdec[rows_of(c), lanes_of(p)])
    ds, decay = {}, {}
    for c in range(n_chunks):
        b_last = b[(c + 1) * CHUNK - 1:(c + 1) * CHUNK, :]
        decay[c] = jnp.exp(b_last)
        kd2 = k[rows_of(c)] * jnp.exp(b_last - b[rows_of(c)])
        for p in range(n_pairs):
            v2 = jnp.concatenate([v_ref[0, rows_of(c), (2 * p + a) * VAL_DIM_B:(2 * p + a + 1) * VAL_DIM_B]
                                  for a in range(2)], axis=0)
            ds[c, p] = _dot_tn(v2, stack_heads(kd2[:, lanes_of(p)]).astype(BF16))
    st_in = {}
    for p in range(n_pairs):
        st = st_sc[p]
        for c in range(n_chunks):
            st_in[c, p] = st.astype(BF16)
            st = st * decay[c][:, lanes_of(p)] + ds[c, p]
        st_sc[p] = st
    for c, p in blocks:
        o_inter = _dot_nt(qm[c, p], st_in[c, p])
        s_cp = jnp.where(causal, sc[c, p], 0.0).astype(BF16)
        o_intra = _dot(s_cp, v_ref[0, rows_of(c), 2 * p * VAL_DIM_B:(2 * p + 2) * VAL_DIM_B])
        for a in range(2):
            h = 2 * p + a
            o_sc[rows_of(c), h * VAL_DIM_B:(h + 1) * VAL_DIM_B] = (
                o_inter[a * CHUNK:(a + 1) * CHUNK]
                + o_intra[a * CHUNK:(a + 1) * CHUNK, a * VAL_DIM_B:(a + 1) * VAL_DIM_B])

    for h in range(N_HEADS_B):
        hs = slice(h * VAL_DIM_B, (h + 1) * VAL_DIM_B)
        o = o_sc[:, hs]
        ms = jnp.mean(o * o, axis=-1, keepdims=True)
        y = o * lax.rsqrt(ms + EPS) * g_ref[:, hs]
        y_ref[0, :, hs] = (y * sg_ref[0, :, hs].astype(F32)).astype(BF16)

    @pl.when(t == pl.num_programs(1) - 1)
    def _():
        for p in range(n_pairs):
            sout_ref[0, p] = st_sc[p].T


def _gla(q, k, v, la, sg, s0, g_gla, *, tg):
    nb, s, _ = q.shape
    assert s % tg == 0 and tg % CHUNK == 0
    n_chunks = tg // CHUNK
    s0p = s0.reshape(s0.shape[0], N_HEADS_B // 2, 2 * KEY_DIM_B, VAL_DIM_B)
    shared = s0p.shape[0] == 1
    wb = min(tg, MXU_WIDTH)
    idx = jnp.arange(wb)
    lbd = ((idx[:, None] // CHUNK == idx[None, :] // CHUNK)
           & (idx[None, :] <= idx[:, None])).astype(BF16)

    def tok(width):
        return pl.BlockSpec((1, tg, width), lambda b, t: (b, t, 0))

    sspec_in = pl.BlockSpec((1,) + s0p.shape[1:], (lambda b, t: (0, 0, 0, 0)) if shared
                            else (lambda b, t: (b, 0, 0, 0)))
    y, s_out = pl.pallas_call(
        functools.partial(_gla_kernel, n_chunks=n_chunks),
        grid=(nb, s // tg),
        in_specs=[tok(KEY_W_B), tok(KEY_W_B), tok(VAL_W_B), tok(KEY_W_B), tok(VAL_W_B), sspec_in,
                  pl.BlockSpec((1, VAL_W_B), lambda b, t: (0, 0)),
                  pl.BlockSpec((wb, wb), lambda b, t: (0, 0))],
        out_specs=[tok(VAL_W_B), pl.BlockSpec((1,) + s0p.shape[1:], lambda b, t: (b, 0, 0, 0))],
        out_shape=[jax.ShapeDtypeStruct((nb, s, VAL_W_B), BF16),
                   jax.ShapeDtypeStruct((nb,) + s0p.shape[1:], F32)],
        scratch_shapes=[pltpu.VMEM((N_HEADS_B // 2, VAL_DIM_B, 2 * KEY_DIM_B), F32),
                        pltpu.VMEM((tg, VAL_W_B), F32)],
        compiler_params=_cparams(2),
        name="gla",
    )(q, k, v, la, sg, s0p, g_gla, lbd)
    return y, s_out.reshape(nb, N_HEADS_B, KEY_DIM_B, VAL_DIM_B)


def _ffn_kernel(x_ref, ya_ref, yb_ref, woa_ref, wob_ref, gn_ref, wg_ref, wu_ref, wd_ref, o_ref,
                *, n_split):
    h = x_ref[...] + _dot(ya_ref[...], woa_ref[...]) + _dot(yb_ref[...], wob_ref[...])
    ms = jnp.mean(h * h, axis=-1, keepdims=True)
    hn = (h * lax.rsqrt(ms + EPS) * gn_ref[...]).astype(BF16)
    tiles = D_FF // MXU_WIDTH
    bounds = [MXU_WIDTH * ((tiles * c + n_split - 1) // n_split) for c in range(n_split + 1)]
    f = None
    for c in range(n_split):
        cs = slice(bounds[c], bounds[c + 1])
        g = _dot(hn, wg_ref[:, cs])
        u = _dot(hn, wu_ref[:, cs])
        a = (g * (1.0 / (1.0 + jnp.exp(-g))) * u).astype(BF16)
        d = _dot(a, wd_ref[cs, :])
        f = d if f is None else f + d
    o_ref[...] = h + f


def _ffn(x, ya, yb, w, *, tm, n_split=2):
    t, d = x.shape
    assert t % tm == 0

    def const(a):
        return pl.BlockSpec(a.shape, lambda i: (0,) * a.ndim)

    consts = [w['w_out_a'], w['w_out_b'], w['g_ffn'], w['w_gate'], w['w_up'], w['w_down']]
    return pl.pallas_call(
        functools.partial(_ffn_kernel, n_split=n_split),
        grid=(t // tm,),
        in_specs=[pl.BlockSpec((tm, d), lambda i: (i, 0)),
                  pl.BlockSpec((tm, VAL_W_A), lambda i: (i, 0)),
                  pl.BlockSpec((tm, VAL_W_B), lambda i: (i, 0))] + [const(a) for a in consts],
        out_specs=pl.BlockSpec((tm, d), lambda i: (i, 0)),
        out_shape=jax.ShapeDtypeStruct((t, d), F32),
        compiler_params=_cparams(1),
        name="ffn",
    )(x, ya, yb, *consts)


def _put_lead_kernel(k_any, v_any, km_ref, vm_ref, ko_ref, vo_ref):
    del k_any, v_any
    ko_ref[0] = km_ref[0]
    vo_ref[0] = vm_ref[0]


def _put_lead_rows(k, v, k_lead, v_lead):
    nb = k.shape[0]
    n_lead = k_lead.shape[1]
    lead = pl.BlockSpec((1, n_lead, HEAD_V_A), lambda b: (0, 0, 0))
    out = pl.BlockSpec((1, n_lead, HEAD_V_A), lambda b: (b, 0, 0))
    return pl.pallas_call(
        _put_lead_kernel,
        grid=(nb,),
        in_specs=[pl.BlockSpec(memory_space=pl.ANY), pl.BlockSpec(memory_space=pl.ANY), lead, lead],
        out_specs=[out, out],
        out_shape=[jax.ShapeDtypeStruct(k.shape, k.dtype), jax.ShapeDtypeStruct(v.shape, v.dtype)],
        input_output_aliases={0: 0, 1: 1},
        compiler_params=_cparams(1),
        name="put_lead",
    )(k, v, k_lead, v_lead)


def _rope_table(pos):
    half = ROT_DIM // 2
    inv_freq = ROPE_THETA ** (-jnp.arange(0, ROT_DIM, 2, dtype=F32) / ROT_DIM)
    ang = pos.astype(F32)[:, None] * inv_freq[None, :]
    cos, sin = jnp.cos(ang), jnp.sin(ang)
    lane = jnp.arange(LANES) % HEAD_DIM_A
    j = lane % half
    c = jnp.where(lane[None, :] < ROT_DIM, cos[:, j], 1.0)
    sa = jnp.where(lane[None, :] < half, -sin[:, j], 0.0)
    sb = jnp.where((lane[None, :] >= half) & (lane[None, :] < ROT_DIM), sin[:, j], 0.0)
    return jnp.concatenate([c, sa, sb], axis=1).astype(F32)


def kernel(x_prompt, x_sample, cache_k_diff, cache_v_diff, state_gla, meta_tokens, norm_mix, w_in,
           w_a2, b_a, q_norm, k_norm, lambda_q1, lambda_k1, lambda_q2, lambda_k2, g_diff, g_gla,
           w_out, norm_ffn, w_ffn_gate, w_ffn_up, w_ffn_down):
    depth = w_in.shape[0]
    assert depth == 1, "single-layer step"
    bsz, seq, d = x_prompt.shape
    db, dt, _ = x_sample.shape
    n_meta = meta_tokens.shape[0]
    past = cache_k_diff.shape[2]
    assert d == D_MODEL and n_meta <= LANES and seq % CHUNK == 0

    w_in0 = w_in[0]
    gid = jnp.arange(MXU_WIDTH) // HEAD_DIM_A
    w = {
        'gmix': norm_mix[0][None, :],
        'w_main': w_in0[:, :N_MAIN].astype(BF16),
        'w_low': jnp.pad(w_in0[:, N_MAIN:], ((0, 0), (0, LANES - GATE_RANK))).astype(BF16),
        'w_a2': jnp.pad(w_a2[0], ((0, LANES - GATE_RANK), (0, 0))).astype(BF16),
        'b_a': b_a[0][None, :],
        'gq': jnp.tile(q_norm[0], QK_W_A // HEAD_DIM_A)[None, :],
        'gk': jnp.tile(k_norm[0], QK_W_A // HEAD_DIM_A)[None, :],
        'gsum': (gid[:, None] == gid[None, :]).astype(BF16),
        'w_out_a': w_out[0][:VAL_W_A].astype(BF16),
        'w_out_b': w_out[0][VAL_W_A:].astype(BF16),
        'g_ffn': norm_ffn[0][None, :],
        'w_gate': w_ffn_gate[0].astype(BF16),
        'w_up': w_ffn_up[0].astype(BF16),
        'w_down': w_ffn_down[0].astype(BF16),
    }
    lam4 = jnp.stack([lambda_q1[0], lambda_k1[0], lambda_q2[0], lambda_k2[0]])
    g_diff2 = g_diff[0][None, :]
    g_gla2 = g_gla[0][None, :]

    rope_m = _rope_table(jnp.arange(n_meta, dtype=jnp.int32))
    (_, kf_m, vf_m, qb_m, kb_m, vb_m, sg_m, la_m) = _proj(
        meta_tokens[None], rope_m, w, tm=n_meta, transposed=False)
    pad_m = ((0, 0), (0, CHUNK - n_meta), (0, 0))
    s_zero = jnp.zeros((1, N_HEADS_B, KEY_DIM_B, VAL_DIM_B), F32)
    _, s_meta = _gla(jnp.pad(qb_m, pad_m), jnp.pad(kb_m, pad_m), jnp.pad(vb_m, pad_m),
                     jnp.pad(la_m, pad_m), jnp.pad(sg_m, pad_m), s_zero, g_gla2, tg=CHUNK)

    tm = 512 if seq % 512 == 0 else seq
    tqk = 512 if seq % 512 == 0 else seq
    tg = 2 * tm if seq % (2 * tm) == 0 else tm
    rope_x = _rope_table(n_meta + jnp.arange(seq, dtype=jnp.int32))
    (qT, k_bf, vT, kf_x, vf_x, qb_x, kb_x, vb_x, sg_x, la_x) = _proj(
        x_prompt, rope_x, w, tm=tg, transposed=True, tq=tqk, tk=tqk, lead_rows=n_meta)
    km = kf_m.reshape(n_meta, QK_W_A).astype(BF16)
    vmT = jnp.pad(vf_m.reshape(n_meta, N_HEADS_A, HEAD_V_A).transpose(1, 2, 0),
                  ((0, 0), (0, 0), (0, LANES - n_meta))).astype(BF16)
    ya_x = _attn(lam4, qT, k_bf, vT, km, vmT, g_diff2, n_meta=n_meta)
    yb_x, s_final = _gla(qb_x, kb_x, vb_x, la_x, sg_x, s_meta, g_gla2, tg=tg)
    y_prompt = _ffn(x_prompt.reshape(bsz * seq, d), ya_x.reshape(bsz * seq, VAL_W_A),
                    yb_x.reshape(bsz * seq, VAL_W_B), w, tm=tm).reshape(bsz, seq, d)
    new_k_p, new_v_p = _put_lead_rows(kf_x, vf_x, kf_m, vf_m)

    ts = db * dt
    rope_s = jnp.tile(_rope_table(past + jnp.arange(dt, dtype=jnp.int32)), (db, 1))
    (q_s, kf_s, vf_s, qb_s, kb_s, vb_s, sg_s, la_s) = _proj(
        x_sample.reshape(1, ts, d), rope_s, w, tm=ts, transposed=False)
    ya_s = _sattn(lam4, q_s.reshape(db, dt, QK_W_A),
                  cache_k_diff.reshape(db, past * N_HEADS_A, HEAD_V_A),
                  cache_v_diff.reshape(db, past * N_HEADS_A, HEAD_V_A),
                  kf_s.reshape(db, dt * N_HEADS_A, HEAD_V_A),
                  vf_s.reshape(db, dt * N_HEADS_A, HEAD_V_A), g_diff2)
    pad_s = ((0, 0), (0, CHUNK - dt), (0, 0))

    def stream(a):
        return jnp.pad(a.reshape(db, dt, a.shape[-1]), pad_s)

    yb_s, s_new = _gla(stream(qb_s), stream(kb_s), stream(vb_s), stream(la_s), stream(sg_s),
                       state_gla[0], g_gla2, tg=CHUNK)
    y_sample = _ffn(x_sample.reshape(ts, d), ya_s.reshape(ts, VAL_W_A),
                    yb_s[:, :dt].reshape(ts, VAL_W_B), w, tm=ts).reshape(db, dt, d)

    return (y_prompt, y_sample,
            new_k_p.reshape(1, bsz, n_meta + seq, N_HEADS_A, 2 * HEAD_DIM_A),
            new_v_p.reshape(1, bsz, n_meta + seq, N_HEADS_A, HEAD_V_A),
            s_final[None],
            kf_s.reshape(1, db, dt, N_HEADS_A, 2 * HEAD_DIM_A),
            vf_s.reshape(1, db, dt, N_HEADS_A, HEAD_V_A),
            s_new[None])
```

```python
import functools
import math

import jax
import jax.numpy as jnp
from jax import lax
from jax.experimental import pallas as pl
from jax.experimental.pallas import tpu as pltpu

F32 = jnp.float32
BF16 = jnp.bfloat16

D_MODEL = 1024
N_HEADS_A = 4
HEAD_DIM_A = 64
HEAD_V_A = 128
QK_W_A = 512
VAL_W_A = 512
ROT_DIM = 16
ROPE_THETA = 500000.0
N_HEADS_B = 4
KEY_DIM_B = 64
VAL_DIM_B = 128
KEY_W_B = 256
VAL_W_B = 512
GATE_RANK = 16
GATE_TAU = 16.0
D_FF = 2816
EPS = 1e-6
CHUNK = 64
LAMBDA_INIT = 0.8 - 0.6 * math.exp(-0.3 * 0)
N_MAIN = 2 * QK_W_A + VAL_W_A + 2 * KEY_W_B + 2 * VAL_W_B
LANES = 128
NEG = -1e30
LOG2E = 1.4426950408889634
MXU_WIDTH = 256
ATTN_COL_BLOCK = MXU_WIDTH
ATTN_HEADS_PER_STEP = 4
ATTN_TILES_PER_TRIP = 4
ATTN_LOOKAHEAD = 4

VMEM_LIMIT = 56 * 1024 * 1024


def _cparams(n_axes):
    return pltpu.CompilerParams(dimension_semantics=("arbitrary",) * n_axes,
                                vmem_limit_bytes=VMEM_LIMIT)


def _dot(a, b):
    return jnp.dot(a, b, preferred_element_type=F32)


def _dot_nt(a, b):
    return lax.dot_general(a, b, (((1,), (1,)), ((), ())), preferred_element_type=F32)


def _dot_tn(a, b):
    return lax.dot_general(a, b, (((0,), (0,)), ((), ())), preferred_element_type=F32)


def _split_bf16(x):
    hi = x.astype(BF16)
    lo = (x - hi.astype(F32)).astype(BF16)
    return hi, lo


def _diff_lambda(lam_ref):
    l4 = lam_ref[...]
    s1 = jnp.sum(l4[0:1] * l4[1:2], axis=-1, keepdims=True)
    s2 = jnp.sum(l4[2:3] * l4[3:4], axis=-1, keepdims=True)
    return jnp.exp(s1) - jnp.exp(s2) + LAMBDA_INIT


def _proj_kernel(x_ref, rope_ref, gmix_ref, wmain_ref, wlow_ref, wa2_ref, ba_ref, gq_ref, gk_ref,
                 gsum_ref, *out_refs, transposed, tq, tk):
    x = x_ref[0]
    tm = x.shape[0]
    ms = jnp.mean(x * x, axis=-1, keepdims=True)
    hn = (x * lax.rsqrt(ms + EPS) * gmix_ref[...]).astype(BF16)

    def seg(lo, hi):
        return _dot(hn, wmain_ref[:, lo:hi])

    cos = rope_ref[:, 0:LANES]
    sin_a = rope_ref[:, LANES:2 * LANES]
    sin_b = rope_ref[:, 2 * LANES:3 * LANES]
    gsum = gsum_ref[...]

    def qknorm_rope(p, g_ref):
        p2 = (p * p).astype(BF16)
        wb = gsum.shape[0]
        ss = jnp.concatenate([_dot(p2[:, c * wb:(c + 1) * wb], gsum)
                              for c in range(QK_W_A // wb)], axis=1)
        pn = p * lax.rsqrt(ss * (1.0 / HEAD_DIM_A) + EPS) * g_ref[...]
        cols = []
        for c in range(QK_W_A // LANES):
            blk = pn[:, c * LANES:(c + 1) * LANES]
            cols.append(blk * cos + pltpu.roll(blk, LANES - ROT_DIM // 2, 1) * sin_a
                        + pltpu.roll(blk, ROT_DIM // 2, 1) * sin_b)
        return jnp.concatenate(cols, axis=1)

    o_qa, o_ka, o_va = 0, QK_W_A, 2 * QK_W_A
    o_qb = o_va + VAL_W_A
    o_kb, o_vb = o_qb + KEY_W_B, o_qb + 2 * KEY_W_B
    o_gb = o_vb + VAL_W_B
    qa = qknorm_rope(seg(o_qa, o_qa + QK_W_A), gq_ref)
    qa = qa * (HEAD_DIM_A ** -0.5 * LOG2E)
    ka = qknorm_rope(seg(o_ka, o_ka + QK_W_A), gk_ref)
    va = seg(o_va, o_va + VAL_W_A)
    qb = seg(o_qb, o_qb + KEY_W_B) * (KEY_DIM_B ** -0.5)
    kb = seg(o_kb, o_kb + KEY_W_B)
    vb = seg(o_vb, o_vb + VAL_W_B)
    gb = seg(o_gb, o_gb + VAL_W_B)
    sg = gb * (1.0 / (1.0 + jnp.exp(-gb)))
    a_low = _dot(hn, wlow_ref[...])
    gate = _dot(a_low.astype(BF16), wa2_ref[...]) + ba_ref[...]
    log_a = (jnp.minimum(gate, 0.0) - jnp.log(1.0 + jnp.exp(-jnp.abs(gate)))) * (1.0 / GATE_TAU)

    if transposed:
        qT_ref, k_ref, vT_ref, kf_ref, vf_ref, qb_ref, kb_ref, vb_ref, sg_ref, la_ref = out_refs
        for r in range(tm // tq):
            qT_ref[0, r] = qa[r * tq:(r + 1) * tq, :].T.astype(BF16)
        for r in range(tm // tk):
            vT_ref[0, r] = va[r * tk:(r + 1) * tk, :].T.astype(BF16)
        k_ref[0] = ka.astype(BF16)
    else:
        q_ref, kf_ref, vf_ref, qb_ref, kb_ref, vb_ref, sg_ref, la_ref = out_refs
        q_ref[0] = qa.astype(BF16)
    for h in range(N_HEADS_A):
        hs = slice(h * HEAD_V_A, (h + 1) * HEAD_V_A)
        kf_ref[0, pl.ds(h, tm, stride=N_HEADS_A), :] = ka[:, hs]
        vf_ref[0, pl.ds(h, tm, stride=N_HEADS_A), :] = va[:, hs]
    qb_ref[0] = qb.astype(BF16)
    kb_ref[0] = kb.astype(BF16)
    vb_ref[0] = vb.astype(BF16)
    sg_ref[0] = sg.astype(BF16)
    la_ref[0] = log_a


def _proj(x, rope, w, *, tm, transposed, tq=256, tk=256, lead_rows=0):
    nb, s, d = x.shape
    assert s % tm == 0
    grid = (s // tm, nb)

    def tok(width, dtype):
        return (jax.ShapeDtypeStruct((nb, s, width), dtype),
                pl.BlockSpec((1, tm, width), lambda i, b: (b, i, 0)))

    def const(a):
        return pl.BlockSpec(a.shape, lambda i, b: (0,) * a.ndim)

    outs = []
    if transposed:
        assert tm % tq == 0 and tm % tk == 0
        outs.append((jax.ShapeDtypeStruct((nb, s // tq, QK_W_A, tq), BF16),
                     pl.BlockSpec((1, tm // tq, QK_W_A, tq), lambda i, b: (b, i, 0, 0))))
        outs.append(tok(QK_W_A, BF16))
        outs.append((jax.ShapeDtypeStruct((nb, s // tk, VAL_W_A, tk), BF16),
                     pl.BlockSpec((1, tm // tk, VAL_W_A, tk), lambda i, b: (b, i, 0, 0))))
    else:
        outs.append(tok(QK_W_A, BF16))
    kv_rows = tm * N_HEADS_A
    kv_shape = jax.ShapeDtypeStruct((nb, (lead_rows + s) * N_HEADS_A, HEAD_V_A), F32)
    if lead_rows:
        kv_spec = pl.BlockSpec((pl.Element(1), pl.Element(kv_rows), pl.Element(HEAD_V_A)),
                               lambda i, b: (b, pl.multiple_of(
                                   lead_rows * N_HEADS_A + i * kv_rows, 8), 0))
    else:
        kv_spec = pl.BlockSpec((1, kv_rows, HEAD_V_A), lambda i, b: (b, i, 0))
    outs += [(kv_shape, kv_spec), (kv_shape, kv_spec), tok(KEY_W_B, BF16), tok(KEY_W_B, BF16),
             tok(VAL_W_B, BF16), tok(VAL_W_B, BF16), tok(KEY_W_B, F32)]
    consts = [w['gmix'], w['w_main'], w['w_low'], w['w_a2'], w['b_a'], w['gq'], w['gk'], w['gsum']]
    return pl.pallas_call(
        functools.partial(_proj_kernel, transposed=transposed, tq=tq, tk=tk),
        grid=grid,
        in_specs=[pl.BlockSpec((1, tm, d), lambda i, b: (b, i, 0)),
                  pl.BlockSpec((tm, 3 * LANES), lambda i, b: (i, 0))] + [const(a) for a in consts],
        out_specs=[o[1] for o in outs],
        out_shape=[o[0] for o in outs],
        compiler_params=_cparams(2),
        name="proj_T" if transposed else "proj_small",
    )(x, rope, *consts)


def _colmax(s):
    rows = s.shape[0]
    slabs = 8 if rows % 64 == 0 else 1
    if slabs > 1:
        s = jnp.max(s.reshape(slabs, rows // slabs, s.shape[1]), axis=0)
    return jnp.max(s, axis=0, keepdims=True)


def _attn_kernel(lam_ref, qT_ref, k_ref, vT_ref, km_ref, vmT_ref, g_ref, o_ref,
                 qz_sc, s_sc, smax_sc, m_sc, acc_sc, *, tq, tk, cw, n_meta, hps):
    i = pl.program_id(2)
    n_cb = 2 * tq // cw
    chains = [(hh, cb) for hh in range(hps) for cb in range(n_cb)]

    def hrows(hh):
        return slice(hh * HEAD_V_A, (hh + 1) * HEAD_V_A)

    for hh in range(hps):
        qT = qT_ref[0, 0, hrows(hh), :]
        row = lax.broadcasted_iota(jnp.int32, qT.shape, 0)
        zero = jnp.zeros_like(qT)
        qz_sc[hh, :, :tq] = jnp.where(row < HEAD_DIM_A, qT, zero)
        qz_sc[hh, :, tq:] = jnp.where(row >= HEAD_DIM_A, qT, zero)

    def ext(vt):
        return jnp.concatenate([vt, jnp.ones((16, vt.shape[1]), BF16)], axis=0)

    def scores(j, hh, cb):
        kt = k_ref[0, pl.ds(pl.multiple_of(j * tk, tk), tk), hrows(hh)]
        return _dot(kt, qz_sc[hh, :, cb * cw:(cb + 1) * cw])

    def update(hh, cb, s, smax, vt):
        cs = slice(cb * cw, (cb + 1) * cw)
        m_old = m_sc[hh, :, cs]
        m_new = jnp.maximum(m_old, smax)
        alpha = jnp.exp2(m_old - m_new)
        p = jnp.exp2(s - m_new).astype(BF16)
        m_sc[hh, :, cs] = m_new
        acc_sc[hh, :, cs] = alpha * acc_sc[hh, :, cs] + _dot(vt, p)

    look = ATTN_LOOKAHEAD
    n_ch = len(chains)
    assert n_ch % look == 0

    def put(slot, s):
        s_sc[slot] = s
        smax_sc[slot] = _colmax(s)

    def prefetch(j, n):
        nn = n + look
        hh, cb = chains[nn % n_ch]
        put(n % look, scores(j + nn // n_ch, hh, cb))

    s_meta = [_dot(km_ref[:, hrows(hh)], qz_sc[hh, :, cb * cw:(cb + 1) * cw]) for hh, cb in chains]
    for n in range(look):
        put(n, scores(0, *chains[n]))
    for n, (hh, cb) in enumerate(chains):
        cs = slice(cb * cw, (cb + 1) * cw)
        s = s_meta[n]
        m0 = jnp.max(s, axis=0, keepdims=True)
        p = jnp.concatenate([jnp.exp2(s - m0).astype(BF16),
                             jnp.zeros((LANES - n_meta, cw), BF16)], axis=0)
        m_sc[hh, :, cs] = m0
        acc_sc[hh, :, cs] = _dot(ext(vmT_ref[hh]), p)

    def full_tile(j):
        for n, (hh, cb) in enumerate(chains):
            s = s_sc[n % look]
            smax = smax_sc[n % look]
            prefetch(j, n)
            update(hh, cb, s, smax, ext(vT_ref[0, j, hrows(hh), :]))

    def body(jj, carry):
        for u in range(ATTN_TILES_PER_TRIP):
            full_tile(ATTN_TILES_PER_TRIP * jj + u)
        return carry

    shift = ATTN_TILES_PER_TRIP.bit_length() - 1
    lax.fori_loop(0, lax.shift_right_logical(i, shift), body, 0)
    done = lax.shift_left(lax.shift_right_logical(i, shift), shift)
    bit = ATTN_TILES_PER_TRIP // 2
    while bit:
        @pl.when((i & bit) != 0)
        def _(done=done, bit=bit):
            for u in range(bit):
                full_tile(done + u)
        done = done + (i & bit)
        bit //= 2

    for n, (hh, cb) in enumerate(chains):
        q0 = (cb * cw) % tq
        rows = min(tk, ((q0 + cw - 1) // CHUNK + 1) * CHUNK)
        s = s_sc[n % look, 0:rows, :]
        if n + look < n_ch:
            prefetch(i, n)
        kr = lax.broadcasted_iota(jnp.int32, s.shape, 0) // CHUNK
        qc = (lax.broadcasted_iota(jnp.int32, s.shape, 1) + q0) // CHUNK
        s = jnp.where(kr <= qc, s, NEG)
        update(hh, cb, s, _colmax(s), ext(vT_ref[0, i, hrows(hh), 0:rows]))

    lam = _diff_lambda(lam_ref)
    for hh in range(hps):
        o = acc_sc[hh, 0:HEAD_V_A, :] * (1.0 / acc_sc[hh, HEAD_V_A:HEAD_V_A + 1, :])
        oT = o[:, :tq] - lam * o[:, tq:]
        ms = jnp.mean(oT * oT, axis=0, keepdims=True)
        on = (oT * lax.rsqrt(ms + EPS)).T
        o_ref[0, :, hrows(hh)] = (on * g_ref[:, hrows(hh)] * (1.0 - LAMBDA_INIT)).astype(BF16)


def _attn(lam4, qT, k, vT, km, vmT, g_diff, *, n_meta):
    nb, nq, _, tq = qT.shape
    _, nk, _, tk = vT.shape
    s = k.shape[1]
    assert tq == tk and n_meta % 16 == 0
    cw = min(ATTN_COL_BLOCK, tq)
    hps = ATTN_HEADS_PER_STEP
    hw = hps * HEAD_V_A
    grid = (nb, N_HEADS_A // hps, nq)
    return pl.pallas_call(
        functools.partial(_attn_kernel, tq=tq, tk=tk, cw=cw, n_meta=n_meta, hps=hps),
        grid=grid,
        in_specs=[pl.BlockSpec(lam4.shape, lambda b, g, i: (0, 0)),
                  pl.BlockSpec((1, 1, hw, tq), lambda b, g, i: (b, i, g, 0)),
                  pl.BlockSpec((1, s, hw), lambda b, g, i: (b, 0, g)),
                  pl.BlockSpec((1, nk, hw, tk), lambda b, g, i: (b, 0, g, 0)),
                  pl.BlockSpec((n_meta, hw), lambda b, g, i: (0, g)),
                  pl.BlockSpec((hps, HEAD_V_A, LANES), lambda b, g, i: (g, 0, 0)),
                  pl.BlockSpec((1, hw), lambda b, g, i: (0, g))],
        out_specs=pl.BlockSpec((1, tq, hw), lambda b, g, i: (b, i, g)),
        out_shape=jax.ShapeDtypeStruct((nb, s, VAL_W_A), BF16),
        scratch_shapes=[pltpu.VMEM((hps, HEAD_V_A, 2 * tq), BF16),
                        pltpu.VMEM((ATTN_LOOKAHEAD, tk, cw), F32),
                        pltpu.VMEM((ATTN_LOOKAHEAD, 1, cw), F32),
                        pltpu.VMEM((hps, 1, 2 * tq), F32),
                        pltpu.VMEM((hps, HEAD_V_A + 16, 2 * tq), F32)],
        compiler_params=_cparams(3),
        name="attn",
    )(lam4, qT, k, vT, km, vmT, g_diff)


def _sattn_kernel(lam_ref, q_ref, ck_ref, cv_ref, kn_ref, vn_ref, g_ref, o_ref, *, p_main, p_all):
    lam = _diff_lambda(lam_ref)
    lane = lax.broadcasted_iota(jnp.int32, (1, LANES), 1)
    for h in range(N_HEADS_A):
        hs = slice(h * HEAD_V_A, (h + 1) * HEAD_V_A)
        qh = q_ref[0, :, hs]
        t = qh.shape[0]

        def head_rows(ref, lo, n):
            return ref[0, pl.ds(lo * N_HEADS_A + h, n, stride=N_HEADS_A), :]

        ks = [head_rows(ck_ref, 0, p_main)]
        vs = [head_rows(cv_ref, 0, p_main)]
        if p_all > p_main:
            ks.append(jnp.concatenate([head_rows(ck_ref, p_main, p_all - p_main),
                                       head_rows(kn_ref, 0, t)], axis=0))
            vs.append(jnp.concatenate([head_rows(cv_ref, p_main, p_all - p_main),
                                       head_rows(vn_ref, 0, t)], axis=0))
        else:
            ks.append(head_rows(kn_ref, 0, t))
            vs.append(head_rows(vn_ref, 0, t))
        ks = [a.astype(BF16) for a in ks]
        vs = [a.astype(BF16) for a in vs]
        zero = jnp.zeros_like(qh)
        q2 = jnp.concatenate([jnp.where(lane < HEAD_DIM_A, qh, zero),
                              jnp.where(lane < HEAD_DIM_A, zero, qh)], axis=0)
        ss = [_dot_nt(q2, kk) for kk in ks]
        m = functools.reduce(jnp.maximum, [jnp.max(a, axis=-1, keepdims=True) for a in ss])
        ps = [jnp.exp2(a - m) for a in ss]
        l = sum(jnp.sum(a, axis=-1, keepdims=True) for a in ps)
        acc = sum(_dot(a.astype(BF16), vv) for a, vv in zip(ps, vs))
        o2 = acc * (1.0 / l)
        o = o2[:t] - lam * o2[t:]
        ms = jnp.mean(o * o, axis=-1, keepdims=True)
        y = o * lax.rsqrt(ms + EPS) * g_ref[:, hs] * (1.0 - LAMBDA_INIT)
        o_ref[0, :, hs] = y.astype(BF16)


def _sattn(lam4, q, ck, cv, kn, vn, g_diff):
    db, t, _ = q.shape
    p_all = ck.shape[1] // N_HEADS_A
    p_main = (p_all // LANES) * LANES
    assert (p_all - p_main + t) % 16 == 0

    def tokspec(n):
        return pl.BlockSpec((1, n, VAL_W_A), lambda b: (b, 0, 0))

    def rowspec(n):
        return pl.BlockSpec((1, n * N_HEADS_A, HEAD_V_A), lambda b: (b, 0, 0))

    return pl.pallas_call(
        functools.partial(_sattn_kernel, p_main=p_main, p_all=p_all),
        grid=(db,),
        in_specs=[pl.BlockSpec(lam4.shape, lambda b: (0, 0)), tokspec(t), rowspec(p_all),
                  rowspec(p_all), rowspec(t), rowspec(t),
                  pl.BlockSpec((1, VAL_W_A), lambda b: (0, 0))],
        out_specs=tokspec(t),
        out_shape=jax.ShapeDtypeStruct((db, t, VAL_W_A), BF16),
        compiler_params=_cparams(1),
        name="sattn",
    )(lam4, q, ck, cv, kn, vn, g_diff)


def _gla_kernel(q_ref, k_ref, v_ref, la_ref, sg_ref, s0_ref, g_ref, lbd_ref, y_ref, sout_ref,
                st_sc, o_sc, *, n_chunks):
    t = pl.program_id(1)
    n_pairs = N_HEADS_B // 2

    @pl.when(t == 0)
    def _():
        for p in range(n_pairs):
            st_sc[p] = s0_ref[0, p].T

    la = la_ref[0]
    hi, lo = _split_bf16(la)
    lbd = lbd_ref[...]
    wb = lbd.shape[0]
    b = jnp.concatenate([_dot(lbd, hi[r * wb:(r + 1) * wb]) + _dot(lbd, lo[r * wb:(r + 1) * wb])
                         for r in range(la.shape[0] // wb)], axis=0)
    q = q_ref[0].astype(F32)
    k = k_ref[0].astype(F32)
    qf = q * jnp.exp(b)
    kdec = (k * jnp.exp(-b)).astype(BF16)
    first = lax.broadcasted_iota(jnp.int32, (1, LANES), 1) < KEY_DIM_B
    tr = lax.broadcasted_iota(jnp.int32, (2 * CHUNK, CHUNK), 0) & (CHUNK - 1)
    tc = lax.broadcasted_iota(jnp.int32, (2 * CHUNK, CHUNK), 1)
    causal = tc <= tr
    blocks = [(c, p) for c in range(n_chunks) for p in range(n_pairs)]

    def rows_of(c):
        return slice(c * CHUNK, (c + 1) * CHUNK)

    def lanes_of(p):
        return slice(p * LANES, (p + 1) * LANES)

    def stack_heads(x):
        return jnp.concatenate([jnp.where(first, x, 0.0), jnp.where(first, 0.0, x)], axis=0)

    qm, sc = {}, {}
    for c, p in blocks:
        qm[c, p] = stack_heads(qf[rows_of(c), lanes_of(p)]).astype(BF16)
        sc[c, p] = _dot_nt(qm[c, p], kdec[rows_of(c), lanes_of(p)])
    ds, decay = {}, {}
    for c in range(n_chunks):
        b_last = b[(c + 1) * CHUNK - 1:(c + 1) * CHUNK, :]
        decay[c] = jnp.exp(b_last)
        kd2 = k[rows_of(c)] * jnp.exp(b_last - b[rows_of(c)])
        for p in range(n_pairs):
            v2 = jnp.concatenate([v_ref[0, rows_of(c), (2 * p + a) * VAL_DIM_B:(2 * p + a + 1) * VAL_DIM_B]
                                  for a in range(2)], axis=0)
            ds[c, p] = _dot_tn(v2, stack_heads(kd2[:, lanes_of(p)]).astype(BF16))
    st_in = {}
    for p in range(n_pairs):
        st = st_sc[p]
        for c in range(n_chunks):
            st_in[c, p] = st.astype(BF16)
            st = st * decay[c][:, lanes_of(p)] + ds[c, p]
        st_sc[p] = st
    for c, p in blocks:
        o_inter = _dot_nt(qm[c, p], st_in[c, p])
        s_cp = jnp.where(causal, sc[c, p], 0.0).astype(BF16)
        o_intra = _dot(s_cp, v_ref[0, rows_of(c), 2 * p * VAL_DIM_B:(2 * p + 2) * VAL_DIM_B])
        for a in range(2):
            h = 2 * p + a
            o_sc[rows_of(c), h * VAL_DIM_B:(h + 1) * VAL_DIM_B] = (
                o_inter[a * CHUNK:(a + 1) * CHUNK]
                + o_intra[a * CHUNK:(a + 1) * CHUNK, a * VAL_DIM_B:(a + 1) * VAL_DIM_B])

    for h in range(N_HEADS_B):
        hs = slice(h * VAL_DIM_B, (h + 1) * VAL_DIM_B)
        o = o_sc[:, hs]
        ms = jnp.mean(o * o, axis=-1, keepdims=True)
        y = o * lax.rsqrt(ms + EPS) * g_ref[:, hs]
        y_ref[0, :, hs] = (y * sg_ref[0, :, hs].astype(F32)).astype(BF16)

    @pl.when(t == pl.num_programs(1) - 1)
    def _():
        for p in range(n_pairs):
            sout_ref[0, p] = st_sc[p].T


def _gla(q, k, v, la, sg, s0, g_gla, *, tg):
    nb, s, _ = q.shape
    assert s % tg == 0 and tg % CHUNK == 0
    n_chunks = tg // CHUNK
    s0p = s0.reshape(s0.shape[0], N_HEADS_B // 2, 2 * KEY_DIM_B, VAL_DIM_B)
    shared = s0p.shape[0] == 1
    wb = min(tg, MXU_WIDTH)
    idx = jnp.arange(wb)
    lbd = ((idx[:, None] // CHUNK == idx[None, :] // CHUNK)
           & (idx[None, :] <= idx[:, None])).astype(BF16)

    def tok(width):
        return pl.BlockSpec((1, tg, width), lambda b, t: (b, t, 0))

    sspec_in = pl.BlockSpec((1,) + s0p.shape[1:], (lambda b, t: (0, 0, 0, 0)) if shared
                            else (lambda b, t: (b, 0, 0, 0)))
    y, s_out = pl.pallas_call(
        functools.partial(_gla_kernel, n_chunks=n_chunks),
        grid=(nb, s // tg),
        in_specs=[tok(KEY_W_B), tok(KEY_W_B), tok(VAL_W_B), tok(KEY_W_B), tok(VAL_W_B), sspec_in,
                  pl.BlockSpec((1, VAL_W_B), lambda b, t: (0, 0)),
                  pl.BlockSpec((wb, wb), lambda b, t: (0, 0))],
        out_specs=[tok(VAL_W_B), pl.BlockSpec((1,) + s0p.shape[1:], lambda b, t: (b, 0, 0, 0))],
        out_shape=[jax.ShapeDtypeStruct((nb, s, VAL_W_B), BF16),
                   jax.ShapeDtypeStruct((nb,) + s0p.shape[1:], F32)],
        scratch_shapes=[pltpu.VMEM((N_HEADS_B // 2, VAL_DIM_B, 2 * KEY_DIM_B), F32),
                        pltpu.VMEM((tg, VAL_W_B), F32)],
        compiler_params=_cparams(2),
        name="gla",
    )(q, k, v, la, sg, s0p, g_gla, lbd)
    return y, s_out.reshape(nb, N_HEADS_B, KEY_DIM_B, VAL_DIM_B)


def _ffn_kernel(x_ref, ya_ref, yb_ref, woa_ref, wob_ref, gn_ref, wg_ref, wu_ref, wd_ref, o_ref,
                *, n_split):
    h = x_ref[...] + _dot(ya_ref[...], woa_ref[...]) + _dot(yb_ref[...], wob_ref[...])
    ms = jnp.mean(h * h, axis=-1, keepdims=True)
    hn = (h * lax.rsqrt(ms + EPS) * gn_ref[...]).astype(BF16)
    tiles = D_FF // MXU_WIDTH
    bounds = [MXU_WIDTH * ((tiles * c + n_split - 1) // n_split) for c in range(n_split + 1)]
    f = None
    for c in range(n_split):
        cs = slice(bounds[c], bounds[c + 1])
        g = _dot(hn, wg_ref[:, cs])
        u = _dot(hn, wu_ref[:, cs])
        a = (g * (1.0 / (1.0 + jnp.exp(-g))) * u).astype(BF16)
        d = _dot(a, wd_ref[cs, :])
        f = d if f is None else f + d
    o_ref[...] = h + f


def _ffn(x, ya, yb, w, *, tm, n_split=2):
    t, d = x.shape
    assert t % tm == 0

    def const(a):
        return pl.BlockSpec(a.shape, lambda i: (0,) * a.ndim)

    consts = [w['w_out_a'], w['w_out_b'], w['g_ffn'], w['w_gate'], w['w_up'], w['w_down']]
    return pl.pallas_call(
        functools.partial(_ffn_kernel, n_split=n_split),
        grid=(t // tm,),
        in_specs=[pl.BlockSpec((tm, d), lambda i: (i, 0)),
                  pl.BlockSpec((tm, VAL_W_A), lambda i: (i, 0)),
                  pl.BlockSpec((tm, VAL_W_B), lambda i: (i, 0))] + [const(a) for a in consts],
        out_specs=pl.BlockSpec((tm, d), lambda i: (i, 0)),
        out_shape=jax.ShapeDtypeStruct((t, d), F32),
        compiler_params=_cparams(1),
        name="ffn",
    )(x, ya, yb, *consts)


def _put_lead_kernel(k_any, v_any, km_ref, vm_ref, ko_ref, vo_ref):
    del k_any, v_any
    ko_ref[0] = km_ref[0]
    vo_ref[0] = vm_ref[0]


def _put_lead_rows(k, v, k_lead, v_lead):
    nb = k.shape[0]
    n_lead = k_lead.shape[1]
    lead = pl.BlockSpec((1, n_lead, HEAD_V_A), lambda b: (0, 0, 0))
    out = pl.BlockSpec((1, n_lead, HEAD_V_A), lambda b: (b, 0, 0))
    return pl.pallas_call(
        _put_lead_kernel,
        grid=(nb,),
        in_specs=[pl.BlockSpec(memory_space=pl.ANY), pl.BlockSpec(memory_space=pl.ANY), lead, lead],
        out_specs=[out, out],
        out_shape=[jax.ShapeDtypeStruct(k.shape, k.dtype), jax.ShapeDtypeStruct(v.shape, v.dtype)],
        input_output_aliases={0: 0, 1: 1},
        compiler_params=_cparams(1),
        name="put_lead",
    )(k, v, k_lead, v_lead)


def _rope_table(pos):
    half = ROT_DIM // 2
    inv_freq = ROPE_THETA ** (-jnp.arange(0, ROT_DIM, 2, dtype=F32) / ROT_DIM)
    ang = pos.astype(F32)[:, None] * inv_freq[None, :]
    cos, sin = jnp.cos(ang), jnp.sin(ang)
    lane = jnp.arange(LANES) % HEAD_DIM_A
    j = lane % half
    c = jnp.where(lane[None, :] < ROT_DIM, cos[:, j], 1.0)
    sa = jnp.where(lane[None, :] < half, -sin[:, j], 0.0)
    sb = jnp.where((lane[None, :] >= half) & (lane[None, :] < ROT_DIM), sin[:, j], 0.0)
    return jnp.concatenate([c, sa, sb], axis=1).astype(F32)


def kernel(x_prompt, x_sample, cache_k_diff, cache_v_diff, state_gla, meta_tokens, norm_mix, w_in,
           w_a2, b_a, q_norm, k_norm, lambda_q1, lambda_k1, lambda_q2, lambda_k2, g_diff, g_gla,
           w_out, norm_ffn, w_ffn_gate, w_ffn_up, w_ffn_down):
    depth = w_in.shape[0]
    assert depth == 1, "single-layer step"
    bsz, seq, d = x_prompt.shape
    db, dt, _ = x_sample.shape
    n_meta = meta_tokens.shape[0]
    past = cache_k_diff.shape[2]
    assert d == D_MODEL and n_meta <= LANES and seq % CHUNK == 0

    w_in0 = w_in[0]
    gid = jnp.arange(MXU_WIDTH) // HEAD_DIM_A
    w = {
        'gmix': norm_mix[0][None, :],
        'w_main': w_in0[:, :N_MAIN].astype(BF16),
        'w_low': jnp.pad(w_in0[:, N_MAIN:], ((0, 0), (0, LANES - GATE_RANK))).astype(BF16),
        'w_a2': jnp.pad(w_a2[0], ((0, LANES - GATE_RANK), (0, 0))).astype(BF16),
        'b_a': b_a[0][None, :],
        'gq': jnp.tile(q_norm[0], QK_W_A // HEAD_DIM_A)[None, :],
        'gk': jnp.tile(k_norm[0], QK_W_A // HEAD_DIM_A)[None, :],
        'gsum': (gid[:, None] == gid[None, :]).astype(BF16),
        'w_out_a': w_out[0][:VAL_W_A].astype(BF16),
        'w_out_b': w_out[0][VAL_W_A:].astype(BF16),
        'g_ffn': norm_ffn[0][None, :],
        'w_gate': w_ffn_gate[0].astype(BF16),
        'w_up': w_ffn_up[0].astype(BF16),
        'w_down': w_ffn_down[0].astype(BF16),
    }
    lam4 = jnp.stack([lambda_q1[0], lambda_k1[0], lambda_q2[0], lambda_k2[0]])
    g_diff2 = g_diff[0][None, :]
    g_gla2 = g_gla[0][None, :]

    rope_m = _rope_table(jnp.arange(n_meta, dtype=jnp.int32))
    (_, kf_m, vf_m, qb_m, kb_m, vb_m, sg_m, la_m) = _proj(
        meta_tokens[None], rope_m, w, tm=n_meta, transposed=False)
    pad_m = ((0, 0), (0, CHUNK - n_meta), (0, 0))
    s_zero = jnp.zeros((1, N_HEADS_B, KEY_DIM_B, VAL_DIM_B), F32)
    _, s_meta = _gla(jnp.pad(qb_m, pad_m), jnp.pad(kb_m, pad_m), jnp.pad(vb_m, pad_m),
                     jnp.pad(la_m, pad_m), jnp.pad(sg_m, pad_m), s_zero, g_gla2, tg=CHUNK)

    tm = 512 if seq % 512 == 0 else seq
    tqk = 512 if seq % 512 == 0 else seq
    tg = 2 * tm if seq % (2 * tm) == 0 else tm
    rope_x = _rope_table(n_meta + jnp.arange(seq, dtype=jnp.int32))
    (qT, k_bf, vT, kf_x, vf_x, qb_x, kb_x, vb_x, sg_x, la_x) = _proj(
        x_prompt, rope_x, w, tm=tg, transposed=True, tq=tqk, tk=tqk, lead_rows=n_meta)
    km = kf_m.reshape(n_meta, QK_W_A).astype(BF16)
    vmT = jnp.pad(vf_m.reshape(n_meta, N_HEADS_A, HEAD_V_A).transpose(1, 2, 0),
                  ((0, 0), (0, 0), (0, LANES - n_meta))).astype(BF16)
    ya_x = _attn(lam4, qT, k_bf, vT, km, vmT, g_diff2, n_meta=n_meta)
    yb_x, s_final = _gla(qb_x, kb_x, vb_x, la_x, sg_x, s_meta, g_gla2, tg=tg)
    y_prompt = _ffn(x_prompt.reshape(bsz * seq, d), ya_x.reshape(bsz * seq, VAL_W_A),
                    yb_x.reshape(bsz * seq, VAL_W_B), w, tm=tm).reshape(bsz, seq, d)
    new_k_p, new_v_p = _put_lead_rows(kf_x, vf_x, kf_m, vf_m)

    ts = db * dt
    rope_s = jnp.tile(_rope_table(past + jnp.arange(dt, dtype=jnp.int32)), (db, 1))
    (q_s, kf_s, vf_s, qb_s, kb_s, vb_s, sg_s, la_s) = _proj(
        x_sample.reshape(1, ts, d), rope_s, w, tm=ts, transposed=False)
    ya_s = _sattn(lam4, q_s.reshape(db, dt, QK_W_A),
                  cache_k_diff.reshape(db, past * N_HEADS_A, HEAD_V_A),
                  cache_v_diff.reshape(db, past * N_HEADS_A, HEAD_V_A),
                  kf_s.reshape(db, dt * N_HEADS_A, HEAD_V_A),
                  vf_s.reshape(db, dt * N_HEADS_A, HEAD_V_A), g_diff2)
    pad_s = ((0, 0), (0, CHUNK - dt), (0, 0))

    def stream(a):
        return jnp.pad(a.reshape(db, dt, a.shape[-1]), pad_s)

    yb_s, s_new = _gla(stream(qb_s), stream(kb_s), stream(vb_s), stream(la_s), stream(sg_s),
                       state_gla[0], g_gla2, tg=CHUNK)
    y_sample = _ffn(x_sample.reshape(ts, d), ya_s.reshape(ts, VAL_W_A),
                    yb_s[:, :dt].reshape(ts, VAL_W_B), w, tm=ts).reshape(db, dt, d)

    return (y_prompt, y_sample,
            new_k_p.reshape(1, bsz, n_meta + seq, N_HEADS_A, 2 * HEAD_DIM_A),
            new_v_p.reshape(1, bsz, n_meta + seq, N_HEADS_A, HEAD_V_A),
            s_final[None],
            kf_s.reshape(1, db, dt, N_HEADS_A, 2 * HEAD_DIM_A),
            vf_s.reshape(1, db, dt, N_HEADS_A, HEAD_V_A),
            s_new[None])
```

```python
import functools
import math

import jax
import jax.numpy as jnp
from jax import lax
from jax.experimental import pallas as pl
from jax.experimental.pallas import tpu as pltpu

F32 = jnp.float32
BF16 = jnp.bfloat16

D_MODEL = 1024
N_HEADS_A = 4
HEAD_DIM_A = 64
HEAD_V_A = 128
QK_W_A = 512
VAL_W_A = 512
ROT_DIM = 16
ROPE_THETA = 500000.0
N_HEADS_B = 4
KEY_DIM_B = 64
VAL_DIM_B = 128
KEY_W_B = 256
VAL_W_B = 512
GATE_RANK = 16
GATE_TAU = 16.0
D_FF = 2816
EPS = 1e-6
CHUNK = 64
LAMBDA_INIT = 0.8 - 0.6 * math.exp(-0.3 * 0)
N_MAIN = 2 * QK_W_A + VAL_W_A + 2 * KEY_W_B + 2 * VAL_W_B
LANES = 128
NEG = -1e30
LOG2E = 1.4426950408889634
MXU_WIDTH = 256
ATTN_COL_BLOCK = MXU_WIDTH
ATTN_HEADS_PER_STEP = 4
ATTN_TILES_PER_TRIP = 4
ATTN_LOOKAHEAD = 4

VMEM_LIMIT = 56 * 1024 * 1024


def _cparams(n_axes):
    return pltpu.CompilerParams(dimension_semantics=("arbitrary",) * n_axes,
                                vmem_limit_bytes=VMEM_LIMIT)


def _dot(a, b):
    return jnp.dot(a, b, preferred_element_type=F32)


def _dot_nt(a, b):
    return lax.dot_general(a, b, (((1,), (1,)), ((), ())), preferred_element_type=F32)


def _dot_tn(a, b):
    return lax.dot_general(a, b, (((0,), (0,)), ((), ())), preferred_element_type=F32)


def _split_bf16(x):
    hi = x.astype(BF16)
    lo = (x - hi.astype(F32)).astype(BF16)
    return hi, lo


def _diff_lambda(lam_ref):
    l4 = lam_ref[...]
    s1 = jnp.sum(l4[0:1] * l4[1:2], axis=-1, keepdims=True)
    s2 = jnp.sum(l4[2:3] * l4[3:4], axis=-1, keepdims=True)
    return jnp.exp(s1) - jnp.exp(s2) + LAMBDA_INIT


def _proj_kernel(x_ref, rope_ref, gmix_ref, wmain_ref, wlow_ref, wa2_ref, ba_ref, gq_ref, gk_ref,
                 gsum_ref, *out_refs, transposed, tq, tk):
    x = x_ref[0]
    tm = x.shape[0]
    ms = jnp.mean(x * x, axis=-1, keepdims=True)
    hn = (x * lax.rsqrt(ms + EPS) * gmix_ref[...]).astype(BF16)

    def seg(lo, hi):
        return _dot(hn, wmain_ref[:, lo:hi])

    cos = rope_ref[:, 0:LANES]
    sin_a = rope_ref[:, LANES:2 * LANES]
    sin_b = rope_ref[:, 2 * LANES:3 * LANES]
    gsum = gsum_ref[...]

    def qknorm_rope(p, g_ref):
        p2 = (p * p).astype(BF16)
        wb = gsum.shape[0]
        ss = jnp.concatenate([_dot(p2[:, c * wb:(c + 1) * wb], gsum)
                              for c in range(QK_W_A // wb)], axis=1)
        pn = p * lax.rsqrt(ss * (1.0 / HEAD_DIM_A) + EPS) * g_ref[...]
        cols = []
        for c in range(QK_W_A // LANES):
            blk = pn[:, c * LANES:(c + 1) * LANES]
            cols.append(blk * cos + pltpu.roll(blk, LANES - ROT_DIM // 2, 1) * sin_a
                        + pltpu.roll(blk, ROT_DIM // 2, 1) * sin_b)
        return jnp.concatenate(cols, axis=1)

    o_qa, o_ka, o_va = 0, QK_W_A, 2 * QK_W_A
    o_qb = o_va + VAL_W_A
    o_kb, o_vb = o_qb + KEY_W_B, o_qb + 2 * KEY_W_B
    o_gb = o_vb + VAL_W_B
    qa = qknorm_rope(seg(o_qa, o_qa + QK_W_A), gq_ref)
    qa = qa * (HEAD_DIM_A ** -0.5 * LOG2E)
    ka = qknorm_rope(seg(o_ka, o_ka + QK_W_A), gk_ref)
    va = seg(o_va, o_va + VAL_W_A)
    qb = seg(o_qb, o_qb + KEY_W_B) * (KEY_DIM_B ** -0.5)
    kb = seg(o_kb, o_kb + KEY_W_B)
    vb = seg(o_vb, o_vb + VAL_W_B)
    gb = seg(o_gb, o_gb + VAL_W_B)
    sg = gb * (1.0 / (1.0 + jnp.exp(-gb)))
    a_low = _dot(hn, wlow_ref[...])
    gate = _dot(a_low.astype(BF16), wa2_ref[...]) + ba_ref[...]
    log_a = (jnp.minimum(gate, 0.0) - jnp.log(1.0 + jnp.exp(-jnp.abs(gate)))) * (1.0 / GATE_TAU)

    if transposed:
        qT_ref, k_ref, vT_ref, kf_ref, vf_ref, qb_ref, kb_ref, vb_ref, sg_ref, la_ref = out_refs
        for r in range(tm // tq):
            qT_ref[0, r] = qa[r * tq:(r + 1) * tq, :].T.astype(BF16)
        for r in range(tm // tk):
            vT_ref[0, r] = va[r * tk:(r + 1) * tk, :].T.astype(BF16)
        k_ref[0] = ka.astype(BF16)
    else:
        q_ref, kf_ref, vf_ref, qb_ref, kb_ref, vb_ref, sg_ref, la_ref = out_refs
        q_ref[0] = qa.astype(BF16)
    for h in range(N_HEADS_A):
        hs = slice(h * HEAD_V_A, (h + 1) * HEAD_V_A)
        kf_ref[0, pl.ds(h, tm, stride=N_HEADS_A), :] = ka[:, hs]
        vf_ref[0, pl.ds(h, tm, stride=N_HEADS_A), :] = va[:, hs]
    qb_ref[0] = qb.astype(BF16)
    kb_ref[0] = kb.astype(BF16)
    vb_ref[0] = vb.astype(BF16)
    sg_ref[0] = sg.astype(BF16)
    la_ref[0] = log_a


def _proj(x, rope, w, *, tm, transposed, tq=256, tk=256, lead_rows=0):
    nb, s, d = x.shape
    assert s % tm == 0
    grid = (s // tm, nb)

    def tok(width, dtype):
        return (jax.ShapeDtypeStruct((nb, s, width), dtype),
                pl.BlockSpec((1, tm, width), lambda i, b: (b, i, 0)))

    def const(a):
        return pl.BlockSpec(a.shape, lambda i, b: (0,) * a.ndim)

    outs = []
    if transposed:
        assert tm % tq == 0 and tm % tk == 0
        outs.append((jax.ShapeDtypeStruct((nb, s // tq, QK_W_A, tq), BF16),
                     pl.BlockSpec((1, tm // tq, QK_W_A, tq), lambda i, b: (b, i, 0, 0))))
        outs.append(tok(QK_W_A, BF16))
        outs.append((jax.ShapeDtypeStruct((nb, s // tk, VAL_W_A, tk), BF16),
                     pl.BlockSpec((1, tm // tk, VAL_W_A, tk), lambda i, b: (b, i, 0, 0))))
    else:
        outs.append(tok(QK_W_A, BF16))
    kv_rows = tm * N_HEADS_A
    kv_shape = jax.ShapeDtypeStruct((nb, (lead_rows + s) * N_HEADS_A, HEAD_V_A), F32)
    if lead_rows:
        kv_spec = pl.BlockSpec((pl.Element(1), pl.Element(kv_rows), pl.Element(HEAD_V_A)),
                               lambda i, b: (b, pl.multiple_of(
                                   lead_rows * N_HEADS_A + i * kv_rows, 8), 0))
    else:
        kv_spec = pl.BlockSpec((1, kv_rows, HEAD_V_A), lambda i, b: (b, i, 0))
    outs += [(kv_shape, kv_spec), (kv_shape, kv_spec), tok(KEY_W_B, BF16), tok(KEY_W_B, BF16),
             tok(VAL_W_B, BF16), tok(VAL_W_B, BF16), tok(KEY_W_B, F32)]
    consts = [w['gmix'], w['w_main'], w['w_low'], w['w_a2'], w['b_a'], w['gq'], w['gk'], w['gsum']]
    return pl.pallas_call(
        functools.partial(_proj_kernel, transposed=transposed, tq=tq, tk=tk),
        grid=grid,
        in_specs=[pl.BlockSpec((1, tm, d), lambda i, b: (b, i, 0)),
                  pl.BlockSpec((tm, 3 * LANES), lambda i, b: (i, 0))] + [const(a) for a in consts],
        out_specs=[o[1] for o in outs],
        out_shape=[o[0] for o in outs],
        compiler_params=_cparams(2),
        name="proj_T" if transposed else "proj_small",
    )(x, rope, *consts)


def _colmax(s):
    rows = s.shape[0]
    slabs = 8 if rows % 64 == 0 else 1
    if slabs > 1:
        s = jnp.max(s.reshape(slabs, rows // slabs, s.shape[1]), axis=0)
    return jnp.max(s, axis=0, keepdims=True)


def _attn_kernel(lam_ref, qT_ref, k_ref, vT_ref, km_ref, vmT_ref, g_ref, o_ref,
                 qz_sc, s_sc, smax_sc, m_sc, acc_sc, *, tq, tk, cw, n_meta, hps):
    i = pl.program_id(2)
    n_cb = 2 * tq // cw
    chains = [(hh, cb) for hh in range(hps) for cb in range(n_cb)]

    def hrows(hh):
        return slice(hh * HEAD_V_A, (hh + 1) * HEAD_V_A)

    for hh in range(hps):
        qT = qT_ref[0, 0, hrows(hh), :]
        row = lax.broadcasted_iota(jnp.int32, qT.shape, 0)
        zero = jnp.zeros_like(qT)
        qz_sc[hh, :, :tq] = jnp.where(row < HEAD_DIM_A, qT, zero)
        qz_sc[hh, :, tq:] = jnp.where(row >= HEAD_DIM_A, qT, zero)

    def ext(vt):
        return jnp.concatenate([vt, jnp.ones((16, vt.shape[1]), BF16)], axis=0)

    def scores(j, hh, cb, rows=tk):
        kt = k_ref[0, pl.ds(pl.multiple_of(j * tk, tk), rows), hrows(hh)]
        return _dot(kt, qz_sc[hh, :, cb * cw:(cb + 1) * cw])

    def update(hh, cb, s, smax, vt):
        cs = slice(cb * cw, (cb + 1) * cw)
        m_old = m_sc[hh, :, cs]
        m_new = jnp.maximum(m_old, smax)
        alpha = jnp.exp2(m_old - m_new)
        p = jnp.exp2(s - m_new).astype(BF16)
        m_sc[hh, :, cs] = m_new
        acc_sc[hh, :, cs] = alpha * acc_sc[hh, :, cs] + _dot(vt, p)

    look = ATTN_LOOKAHEAD
    n_ch = len(chains)
    assert n_ch % look == 0

    def put(slot, s):
        s_sc[slot] = s
        smax_sc[slot] = _colmax(s)

    def prefetch(j, n):
        nn = n + look
        hh, cb = chains[nn % n_ch]
        put(n % look, scores(j + nn // n_ch, hh, cb))

    s_meta = [_dot(km_ref[:, hrows(hh)], qz_sc[hh, :, cb * cw:(cb + 1) * cw]) for hh, cb in chains]
    for n in range(look):
        put(n, scores(0, *chains[n]))
    for n, (hh, cb) in enumerate(chains):
        cs = slice(cb * cw, (cb + 1) * cw)
        s = s_meta[n]
        m0 = jnp.max(s, axis=0, keepdims=True)
        p = jnp.concatenate([jnp.exp2(s - m0).astype(BF16),
                             jnp.zeros((LANES - n_meta, cw), BF16)], axis=0)
        m_sc[hh, :, cs] = m0
        acc_sc[hh, :, cs] = _dot(ext(vmT_ref[hh]), p)

    def full_tile(j):
        for n, (hh, cb) in enumerate(chains):
            s = s_sc[n % look]
            smax = smax_sc[n % look]
            prefetch(j, n)
            update(hh, cb, s, smax, ext(vT_ref[0, j, hrows(hh), :]))

    def body(jj, carry):
        for u in range(ATTN_TILES_PER_TRIP):
            full_tile(ATTN_TILES_PER_TRIP * jj + u)
        return carry

    shift = ATTN_TILES_PER_TRIP.bit_length() - 1
    lax.fori_loop(0, lax.shift_right_logical(i, shift), body, 0)
    done = lax.shift_left(lax.shift_right_logical(i, shift), shift)
    bit = ATTN_TILES_PER_TRIP // 2
    while bit:
        @pl.when((i & bit) != 0)
        def _(done=done, bit=bit):
            for u in range(bit):
                full_tile(done + u)
        done = done + (i & bit)
        bit //= 2

    def diag_rows(cb):
        return min(tk, (((cb * cw) % tq + cw - 1) // CHUNK + 1) * CHUNK)

    for n, (hh, cb) in enumerate(chains):
        rows = diag_rows(cb)
        s = s_sc[n % look, 0:rows, :]
        if n + look < n_ch:
            hh2, cb2 = chains[n + look]
            s_sc[n % look, 0:diag_rows(cb2), :] = scores(i, hh2, cb2, diag_rows(cb2))
        kr = lax.broadcasted_iota(jnp.int32, s.shape, 0) // CHUNK
        qc = (lax.broadcasted_iota(jnp.int32, s.shape, 1) + (cb * cw) % tq) // CHUNK
        s = jnp.where(kr <= qc, s, NEG)
        update(hh, cb, s, _colmax(s), ext(vT_ref[0, i, hrows(hh), 0:rows]))

    lam = _diff_lambda(lam_ref)
    for hh in range(hps):
        o = acc_sc[hh, 0:HEAD_V_A, :] * (1.0 / acc_sc[hh, HEAD_V_A:HEAD_V_A + 1, :])
        oT = o[:, :tq] - lam * o[:, tq:]
        ms = jnp.mean(oT * oT, axis=0, keepdims=True)
        on = (oT * lax.rsqrt(ms + EPS)).T
        o_ref[0, :, hrows(hh)] = (on * g_ref[:, hrows(hh)] * (1.0 - LAMBDA_INIT)).astype(BF16)


def _attn(lam4, qT, k, vT, km, vmT, g_diff, *, n_meta):
    nb, nq, _, tq = qT.shape
    _, nk, _, tk = vT.shape
    s = k.shape[1]
    assert tq == tk and n_meta % 16 == 0
    cw = min(ATTN_COL_BLOCK, tq)
    hps = ATTN_HEADS_PER_STEP
    hw = hps * HEAD_V_A
    grid = (nb, N_HEADS_A // hps, nq)
    return pl.pallas_call(
        functools.partial(_attn_kernel, tq=tq, tk=tk, cw=cw, n_meta=n_meta, hps=hps),
        grid=grid,
        in_specs=[pl.BlockSpec(lam4.shape, lambda b, g, i: (0, 0)),
                  pl.BlockSpec((1, 1, hw, tq), lambda b, g, i: (b, i, g, 0)),
                  pl.BlockSpec((1, s, hw), lambda b, g, i: (b, 0, g)),
                  pl.BlockSpec((1, nk, hw, tk), lambda b, g, i: (b, 0, g, 0)),
                  pl.BlockSpec((n_meta, hw), lambda b, g, i: (0, g)),
                  pl.BlockSpec((hps, HEAD_V_A, LANES), lambda b, g, i: (g, 0, 0)),
                  pl.BlockSpec((1, hw), lambda b, g, i: (0, g))],
        out_specs=pl.BlockSpec((1, tq, hw), lambda b, g, i: (b, i, g)),
        out_shape=jax.ShapeDtypeStruct((nb, s, VAL_W_A), BF16),
        scratch_shapes=[pltpu.VMEM((hps, HEAD_V_A, 2 * tq), BF16),
                        pltpu.VMEM((ATTN_LOOKAHEAD, tk, cw), F32),
                        pltpu.VMEM((ATTN_LOOKAHEAD, 1, cw), F32),
                        pltpu.VMEM((hps, 1, 2 * tq), F32),
                        pltpu.VMEM((hps, HEAD_V_A + 16, 2 * tq), F32)],
        compiler_params=_cparams(3),
        name="attn",
    )(lam4, qT, k, vT, km, vmT, g_diff)


def _sattn_kernel(lam_ref, q_ref, ck_ref, cv_ref, kn_ref, vn_ref, g_ref, o_ref, *, p_main, p_all):
    lam = _diff_lambda(lam_ref)
    lane = lax.broadcasted_iota(jnp.int32, (1, LANES), 1)
    for h in range(N_HEADS_A):
        hs = slice(h * HEAD_V_A, (h + 1) * HEAD_V_A)
        qh = q_ref[0, :, hs]
        t = qh.shape[0]

        def head_rows(ref, lo, n):
            return ref[0, pl.ds(lo * N_HEADS_A + h, n, stride=N_HEADS_A), :]

        ks = [head_rows(ck_ref, 0, p_main)]
        vs = [head_rows(cv_ref, 0, p_main)]
        if p_all > p_main:
            ks.append(jnp.concatenate([head_rows(ck_ref, p_main, p_all - p_main),
                                       head_rows(kn_ref, 0, t)], axis=0))
            vs.append(jnp.concatenate([head_rows(cv_ref, p_main, p_all - p_main),
                                       head_rows(vn_ref, 0, t)], axis=0))
        else:
            ks.append(head_rows(kn_ref, 0, t))
            vs.append(head_rows(vn_ref, 0, t))
        ks = [a.astype(BF16) for a in ks]
        vs = [a.astype(BF16) for a in vs]
        zero = jnp.zeros_like(qh)
        q2 = jnp.concatenate([jnp.where(lane < HEAD_DIM_A, qh, zero),
                              jnp.where(lane < HEAD_DIM_A, zero, qh)], axis=0)
        ss = [_dot_nt(q2, kk) for kk in ks]
        m = functools.reduce(jnp.maximum, [jnp.max(a, axis=-1, keepdims=True) for a in ss])
        ps = [jnp.exp2(a - m) for a in ss]
        l = sum(jnp.sum(a, axis=-1, keepdims=True) for a in ps)
        acc = sum(_dot(a.astype(BF16), vv) for a, vv in zip(ps, vs))
        o2 = acc * (1.0 / l)
        o = o2[:t] - lam * o2[t:]
        ms = jnp.mean(o * o, axis=-1, keepdims=True)
        y = o * lax.rsqrt(ms + EPS) * g_ref[:, hs] * (1.0 - LAMBDA_INIT)
        o_ref[0, :, hs] = y.astype(BF16)


def _sattn(lam4, q, ck, cv, kn, vn, g_diff):
    db, t, _ = q.shape
    p_all = ck.shape[1] // N_HEADS_A
    p_main = (p_all // LANES) * LANES
    assert (p_all - p_main + t) % 16 == 0

    def tokspec(n):
        return pl.BlockSpec((1, n, VAL_W_A), lambda b: (b, 0, 0))

    def rowspec(n):
        return pl.BlockSpec((1, n * N_HEADS_A, HEAD_V_A), lambda b: (b, 0, 0))

    return pl.pallas_call(
        functools.partial(_sattn_kernel, p_main=p_main, p_all=p_all),
        grid=(db,),
        in_specs=[pl.BlockSpec(lam4.shape, lambda b: (0, 0)), tokspec(t), rowspec(p_all),
                  rowspec(p_all), rowspec(t), rowspec(t),
                  pl.BlockSpec((1, VAL_W_A), lambda b: (0, 0))],
        out_specs=tokspec(t),
        out_shape=jax.ShapeDtypeStruct((db, t, VAL_W_A), BF16),
        compiler_params=_cparams(1),
        name="sattn",
    )(lam4, q, ck, cv, kn, vn, g_diff)


def _gla_kernel(q_ref, k_ref, v_ref, la_ref, sg_ref, s0_ref, g_ref, lbd_ref, y_ref, sout_ref,
                st_sc, o_sc, *, n_chunks):
    t = pl.program_id(1)
    n_pairs = N_HEADS_B // 2

    @pl.when(t == 0)
    def _():
        for p in range(n_pairs):
            st_sc[p] = s0_ref[0, p].T

    la = la_ref[0]
    hi, lo = _split_bf16(la)
    lbd = lbd_ref[...]
    wb = lbd.shape[0]
    b = jnp.concatenate([_dot(lbd, hi[r * wb:(r + 1) * wb]) + _dot(lbd, lo[r * wb:(r + 1) * wb])
                         for r in range(la.shape[0] // wb)], axis=0)
    q = q_ref[0].astype(F32)
    k = k_ref[0].astype(F32)
    qf = q * jnp.exp(b)
    kdec = (k * jnp.exp(-b)).astype(BF16)
    first = lax.broadcasted_iota(jnp.int32, (1, LANES), 1) < KEY_DIM_B
    tr = lax.broadcasted_iota(jnp.int32, (2 * CHUNK, CHUNK), 0) & (CHUNK - 1)
    tc = lax.broadcasted_iota(jnp.int32, (2 * CHUNK, CHUNK), 1)
    causal = tc <= tr
    blocks = [(c, p) for c in range(n_chunks) for p in range(n_pairs)]

    def rows_of(c):
        return slice(c * CHUNK, (c + 1) * CHUNK)

    def lanes_of(p):
        return slice(p * LANES, (p + 1) * LANES)

    def stack_heads(x):
        return jnp.concatenate([jnp.where(first, x, 0.0), jnp.where(first, 0.0, x)], axis=0)

    qm, sc = {}, {}
    for c, p in blocks:
        qm[c, p] = stack_heads(qf[rows_of(c), lanes_of(p)]).astype(BF16)
        sc[c, p] = _dot_nt(qm[c, p], kdec[rows_of(c), lanes_of(p)])
    ds, decay = {}, {}
    for c in range(n_chunks):
        b_last = b[(c + 1) * CHUNK - 1:(c + 1) * CHUNK, :]
        decay[c] = jnp.exp(b_last)
        kd2 = k[rows_of(c)] * jnp.exp(b_last - b[rows_of(c)])
        for p in range(n_pairs):
            v2 = jnp.concatenate([v_ref[0, rows_of(c), (2 * p + a) * VAL_DIM_B:(2 * p + a + 1) * VAL_DIM_B]
                                  for a in range(2)], axis=0)
            ds[c, p] = _dot_tn(v2, stack_heads(kd2[:, lanes_of(p)]).astype(BF16))
    st_in = {}
    for p in range(n_pairs):
        st = st_sc[p]
        for c in range(n_chunks):
            st_in[c, p] = st.astype(BF16)
            st = st * decay[c][:, lanes_of(p)] + ds[c, p]
        st_sc[p] = st
    for c, p in blocks:
        o_inter = _dot_nt(qm[c, p], st_in[c, p])
        s_cp = jnp.where(causal, sc[c, p], 0.0).astype(BF16)
        o_intra = _dot(s_cp, v_ref[0, rows_of(c), 2 * p * VAL_DIM_B:(2 * p + 2) * VAL_DIM_B])
        for a in range(2):
            h = 2 * p + a
            o_sc[rows_of(c), h * VAL_DIM_B:(h + 1) * VAL_DIM_B] = (
                o_inter[a * CHUNK:(a + 1) * CHUNK]
                + o_intra[a * CHUNK:(a + 1) * CHUNK, a * VAL_DIM_B:(a + 1) * VAL_DIM_B])

    for h in range(N_HEADS_B):
        hs = slice(h * VAL_DIM_B, (h + 1) * VAL_DIM_B)
        o = o_sc[:, hs]
        ms = jnp.mean(o * o, axis=-1, keepdims=True)
        y = o * lax.rsqrt(ms + EPS) * g_ref[:, hs]
        y_ref[0, :, hs] = (y * sg_ref[0, :, hs].astype(F32)).astype(BF16)

    @pl.when(t == pl.num_programs(1) - 1)
    def _():
        for p in range(n_pairs):
            sout_ref[0, p] = st_sc[p].T


def _gla(q, k, v, la, sg, s0, g_gla, *, tg):
    nb, s, _ = q.shape
    assert s % tg == 0 and tg % CHUNK == 0
    n_chunks = tg // CHUNK
    s0p = s0.reshape(s0.shape[0], N_HEADS_B // 2, 2 * KEY_DIM_B, VAL_DIM_B)
    shared = s0p.shape[0] == 1
    wb = min(tg, MXU_WIDTH)
    idx = jnp.arange(wb)
    lbd = ((idx[:, None] // CHUNK == idx[None, :] // CHUNK)
           & (idx[None, :] <= idx[:, None])).astype(BF16)

    def tok(width):
        return pl.BlockSpec((1, tg, width), lambda b, t: (b, t, 0))

    sspec_in = pl.BlockSpec((1,) + s0p.shape[1:], (lambda b, t: (0, 0, 0, 0)) if shared
                            else (lambda b, t: (b, 0, 0, 0)))
    y, s_out = pl.pallas_call(
        functools.partial(_gla_kernel, n_chunks=n_chunks),
        grid=(nb, s // tg),
        in_specs=[tok(KEY_W_B), tok(KEY_W_B), tok(VAL_W_B), tok(KEY_W_B), tok(VAL_W_B), sspec_in,
                  pl.BlockSpec((1, VAL_W_B), lambda b, t: (0, 0)),
                  pl.BlockSpec((wb, wb), lambda b, t: (0, 0))],
        out_specs=[tok(VAL_W_B), pl.BlockSpec((1,) + s0p.shape[1:], lambda b, t: (b, 0, 0, 0))],
        out_shape=[jax.ShapeDtypeStruct((nb, s, VAL_W_B), BF16),
                   jax.ShapeDtypeStruct((nb,) + s0p.shape[1:], F32)],
        scratch_shapes=[pltpu.VMEM((N_HEADS_B // 2, VAL_DIM_B, 2 * KEY_DIM_B), F32),
                        pltpu.VMEM((tg, VAL_W_B), F32)],
        compiler_params=_cparams(2),
        name="gla",
    )(q, k, v, la, sg, s0p, g_gla, lbd)
    return y, s_out.reshape(nb, N_HEADS_B, KEY_DIM_B, VAL_DIM_B)


def _ffn_kernel(x_ref, ya_ref, yb_ref, woa_ref, wob_ref, gn_ref, wg_ref, wu_ref, wd_ref, o_ref,
                *, n_split):
    h = x_ref[...] + _dot(ya_ref[...], woa_ref[...]) + _dot(yb_ref[...], wob_ref[...])
    ms = jnp.mean(h * h, axis=-1, keepdims=True)
    hn = (h * lax.rsqrt(ms + EPS) * gn_ref[...]).astype(BF16)
    tiles = D_FF // MXU_WIDTH
    bounds = [MXU_WIDTH * ((tiles * c + n_split - 1) // n_split) for c in range(n_split + 1)]
    f = None
    for c in range(n_split):
        cs = slice(bounds[c], bounds[c + 1])
        g = _dot(hn, wg_ref[:, cs])
        u = _dot(hn, wu_ref[:, cs])
        a = (g * (1.0 / (1.0 + jnp.exp(-g))) * u).astype(BF16)
        d = _dot(a, wd_ref[cs, :])
        f = d if f is None else f + d
    o_ref[...] = h + f


def _ffn(x, ya, yb, w, *, tm, n_split=2):
    t, d = x.shape
    assert t % tm == 0

    def const(a):
        return pl.BlockSpec(a.shape, lambda i: (0,) * a.ndim, pipeline_mode=pl.Buffered(1))

    consts = [w['w_out_a'], w['w_out_b'], w['g_ffn'], w['w_gate'], w['w_up'], w['w_down']]
    return pl.pallas_call(
        functools.partial(_ffn_kernel, n_split=n_split),
        grid=(t // tm,),
        in_specs=[pl.BlockSpec((tm, d), lambda i: (i, 0)),
                  pl.BlockSpec((tm, VAL_W_A), lambda i: (i, 0)),
                  pl.BlockSpec((tm, VAL_W_B), lambda i: (i, 0))] + [const(a) for a in consts],
        out_specs=pl.BlockSpec((tm, d), lambda i: (i, 0)),
        out_shape=jax.ShapeDtypeStruct((t, d), F32),
        compiler_params=_cparams(1),
        name="ffn",
    )(x, ya, yb, *consts)


def _put_lead_kernel(k_any, v_any, km_ref, vm_ref, ko_ref, vo_ref):
    del k_any, v_any
    ko_ref[0] = km_ref[0]
    vo_ref[0] = vm_ref[0]


def _put_lead_rows(k, v, k_lead, v_lead):
    nb = k.shape[0]
    n_lead = k_lead.shape[1]
    lead = pl.BlockSpec((1, n_lead, HEAD_V_A), lambda b: (0, 0, 0))
    out = pl.BlockSpec((1, n_lead, HEAD_V_A), lambda b: (b, 0, 0))
    return pl.pallas_call(
        _put_lead_kernel,
        grid=(nb,),
        in_specs=[pl.BlockSpec(memory_space=pl.ANY), pl.BlockSpec(memory_space=pl.ANY), lead, lead],
        out_specs=[out, out],
        out_shape=[jax.ShapeDtypeStruct(k.shape, k.dtype), jax.ShapeDtypeStruct(v.shape, v.dtype)],
        input_output_aliases={0: 0, 1: 1},
        compiler_params=_cparams(1),
        name="put_lead",
    )(k, v, k_lead, v_lead)


def _rope_table(pos):
    half = ROT_DIM // 2
    inv_freq = ROPE_THETA ** (-jnp.arange(0, ROT_DIM, 2, dtype=F32) / ROT_DIM)
    ang = pos.astype(F32)[:, None] * inv_freq[None, :]
    cos, sin = jnp.cos(ang), jnp.sin(ang)
    lane = jnp.arange(LANES) % HEAD_DIM_A
    j = lane % half
    c = jnp.where(lane[None, :] < ROT_DIM, cos[:, j], 1.0)
    sa = jnp.where(lane[None, :] < half, -sin[:, j], 0.0)
    sb = jnp.where((lane[None, :] >= half) & (lane[None, :] < ROT_DIM), sin[:, j], 0.0)
    return jnp.concatenate([c, sa, sb], axis=1).astype(F32)


def kernel(x_prompt, x_sample, cache_k_diff, cache_v_diff, state_gla, meta_tokens, norm_mix, w_in,
           w_a2, b_a, q_norm, k_norm, lambda_q1, lambda_k1, lambda_q2, lambda_k2, g_diff, g_gla,
           w_out, norm_ffn, w_ffn_gate, w_ffn_up, w_ffn_down):
    depth = w_in.shape[0]
    assert depth == 1, "single-layer step"
    bsz, seq, d = x_prompt.shape
    db, dt, _ = x_sample.shape
    n_meta = meta_tokens.shape[0]
    past = cache_k_diff.shape[2]
    assert d == D_MODEL and n_meta <= LANES and seq % CHUNK == 0

    w_in0 = w_in[0]
    gid = jnp.arange(MXU_WIDTH) // HEAD_DIM_A
    w = {
        'gmix': norm_mix[0][None, :],
        'w_main': w_in0[:, :N_MAIN].astype(BF16),
        'w_low': jnp.pad(w_in0[:, N_MAIN:], ((0, 0), (0, LANES - GATE_RANK))).astype(BF16),
        'w_a2': jnp.pad(w_a2[0], ((0, LANES - GATE_RANK), (0, 0))).astype(BF16),
        'b_a': b_a[0][None, :],
        'gq': jnp.tile(q_norm[0], QK_W_A // HEAD_DIM_A)[None, :],
        'gk': jnp.tile(k_norm[0], QK_W_A // HEAD_DIM_A)[None, :],
        'gsum': (gid[:, None] == gid[None, :]).astype(BF16),
        'w_out_a': w_out[0][:VAL_W_A].astype(BF16),
        'w_out_b': w_out[0][VAL_W_A:].astype(BF16),
        'g_ffn': norm_ffn[0][None, :],
        'w_gate': w_ffn_gate[0].astype(BF16),
        'w_up': w_ffn_up[0].astype(BF16),
        'w_down': w_ffn_down[0].astype(BF16),
    }
    lam4 = jnp.stack([lambda_q1[0], lambda_k1[0], lambda_q2[0], lambda_k2[0]])
    g_diff2 = g_diff[0][None, :]
    g_gla2 = g_gla[0][None, :]

    rope_m = _rope_table(jnp.arange(n_meta, dtype=jnp.int32))
    (_, kf_m, vf_m, qb_m, kb_m, vb_m, sg_m, la_m) = _proj(
        meta_tokens[None], rope_m, w, tm=n_meta, transposed=False)
    pad_m = ((0, 0), (0, CHUNK - n_meta), (0, 0))
    s_zero = jnp.zeros((1, N_HEADS_B, KEY_DIM_B, VAL_DIM_B), F32)
    _, s_meta = _gla(jnp.pad(qb_m, pad_m), jnp.pad(kb_m, pad_m), jnp.pad(vb_m, pad_m),
                     jnp.pad(la_m, pad_m), jnp.pad(sg_m, pad_m), s_zero, g_gla2, tg=CHUNK)

    tm = 512 if seq % 512 == 0 else seq
    tqk = 512 if seq % 512 == 0 else seq
    tg = 2 * tm if seq % (2 * tm) == 0 else tm
    rope_x = _rope_table(n_meta + jnp.arange(seq, dtype=jnp.int32))
    (qT, k_bf, vT, kf_x, vf_x, qb_x, kb_x, vb_x, sg_x, la_x) = _proj(
        x_prompt, rope_x, w, tm=tg, transposed=True, tq=tqk, tk=tqk, lead_rows=n_meta)
    km = kf_m.reshape(n_meta, QK_W_A).astype(BF16)
    vmT = jnp.pad(vf_m.reshape(n_meta, N_HEADS_A, HEAD_V_A).transpose(1, 2, 0),
                  ((0, 0), (0, 0), (0, LANES - n_meta))).astype(BF16)
    ya_x = _attn(lam4, qT, k_bf, vT, km, vmT, g_diff2, n_meta=n_meta)
    yb_x, s_final = _gla(qb_x, kb_x, vb_x, la_x, sg_x, s_meta, g_gla2, tg=tg)
    y_prompt = _ffn(x_prompt.reshape(bsz * seq, d), ya_x.reshape(bsz * seq, VAL_W_A),
                    yb_x.reshape(bsz * seq, VAL_W_B), w, tm=tg).reshape(bsz, seq, d)
    new_k_p, new_v_p = _put_lead_rows(kf_x, vf_x, kf_m, vf_m)

    ts = db * dt
    rope_s = jnp.tile(_rope_table(past + jnp.arange(dt, dtype=jnp.int32)), (db, 1))
    (q_s, kf_s, vf_s, qb_s, kb_s, vb_s, sg_s, la_s) = _proj(
        x_sample.reshape(1, ts, d), rope_s, w, tm=ts, transposed=False)
    ya_s = _sattn(lam4, q_s.reshape(db, dt, QK_W_A),
                  cache_k_diff.reshape(db, past * N_HEADS_A, HEAD_V_A),
                  cache_v_diff.reshape(db, past * N_HEADS_A, HEAD_V_A),
                  kf_s.reshape(db, dt * N_HEADS_A, HEAD_V_A),
                  vf_s.reshape(db, dt * N_HEADS_A, HEAD_V_A), g_diff2)
    pad_s = ((0, 0), (0, CHUNK - dt), (0, 0))

    def stream(a):
        return jnp.pad(a.reshape(db, dt, a.shape[-1]), pad_s)

    yb_s, s_new = _gla(stream(qb_s), stream(kb_s), stream(vb_s), stream(la_s), stream(sg_s),
                       state_gla[0], g_gla2, tg=CHUNK)
    y_sample = _ffn(x_sample.reshape(ts, d), ya_s.reshape(ts, VAL_W_A),
                    yb_s[:, :dt].reshape(ts, VAL_W_B), w, tm=ts).reshape(db, dt, d)

    return (y_prompt, y_sample,
            new_k_p.reshape(1, bsz, n_meta + seq, N_HEADS_A, 2 * HEAD_DIM_A),
            new_v_p.reshape(1, bsz, n_meta + seq, N_HEADS_A, HEAD_V_A),
            s_final[None],
            kf_s.reshape(1, db, dt, N_HEADS_A, 2 * HEAD_DIM_A),
            vf_s.reshape(1, db, dt, N_HEADS_A, HEAD_V_A),
            s_new[None])
```

```python
import functools
import math

import jax
import jax.numpy as jnp
from jax import lax
from jax.experimental import pallas as pl
from jax.experimental.pallas import tpu as pltpu

F32 = jnp.float32
BF16 = jnp.bfloat16

D_MODEL = 1024
N_HEADS_A = 4
HEAD_DIM_A = 64
HEAD_V_A = 128
QK_W_A = 512
VAL_W_A = 512
ROT_DIM = 16
ROPE_THETA = 500000.0
N_HEADS_B = 4
KEY_DIM_B = 64
VAL_DIM_B = 128
KEY_W_B = 256
VAL_W_B = 512
GATE_RANK = 16
GATE_TAU = 16.0
D_FF = 2816
EPS = 1e-6
CHUNK = 64
LAMBDA_INIT = 0.8 - 0.6 * math.exp(-0.3 * 0)
N_MAIN = 2 * QK_W_A + VAL_W_A + 2 * KEY_W_B + 2 * VAL_W_B
LANES = 128
BF16_ROWS = 16
NEG = -1e30
LOG2E = 1.4426950408889634
MXU_WIDTH = 256
ATTN_COL_BLOCK = MXU_WIDTH
ATTN_HEADS_PER_STEP = 4
ATTN_TILES_PER_TRIP = 4
ATTN_LOOKAHEAD = 4

VMEM_LIMIT = 56 * 1024 * 1024


def _cparams(n_axes):
    return pltpu.CompilerParams(dimension_semantics=("arbitrary",) * n_axes,
                                vmem_limit_bytes=VMEM_LIMIT)


def _dot(a, b):
    return jnp.dot(a, b, preferred_element_type=F32)


def _dot_nt(a, b):
    return lax.dot_general(a, b, (((1,), (1,)), ((), ())), preferred_element_type=F32)


def _dot_tn(a, b):
    return lax.dot_general(a, b, (((0,), (0,)), ((), ())), preferred_element_type=F32)


def _split_bf16(x):
    hi = x.astype(BF16)
    lo = (x - hi.astype(F32)).astype(BF16)
    return hi, lo


def _diff_lambda(lam_ref):
    l4 = lam_ref[...]
    s1 = jnp.sum(l4[0:1] * l4[1:2], axis=-1, keepdims=True)
    s2 = jnp.sum(l4[2:3] * l4[3:4], axis=-1, keepdims=True)
    return jnp.exp(s1) - jnp.exp(s2) + LAMBDA_INIT


def _proj_kernel(x_ref, rope_ref, gmix_ref, wmain_ref, wlow_ref, wa2_ref, ba_ref, gq_ref, gk_ref,
                 gsum_ref, *out_refs, transposed, tq, tk):
    x = x_ref[0]
    tm = x.shape[0]
    ms = jnp.mean(x * x, axis=-1, keepdims=True)
    hn = (x * lax.rsqrt(ms + EPS) * gmix_ref[...]).astype(BF16)

    def seg(lo, hi):
        return _dot(hn, wmain_ref[:, lo:hi])

    cos = rope_ref[:, 0:LANES]
    sin_a = rope_ref[:, LANES:2 * LANES]
    sin_b = rope_ref[:, 2 * LANES:3 * LANES]
    gsum = gsum_ref[...]

    def qknorm_rope(p, g_ref):
        p2 = (p * p).astype(BF16)
        wb = gsum.shape[0]
        ss = jnp.concatenate([_dot(p2[:, c * wb:(c + 1) * wb], gsum)
                              for c in range(QK_W_A // wb)], axis=1)
        pn = p * lax.rsqrt(ss * (1.0 / HEAD_DIM_A) + EPS) * g_ref[...]
        cols = []
        for c in range(QK_W_A // LANES):
            blk = pn[:, c * LANES:(c + 1) * LANES]
            cols.append(blk * cos + pltpu.roll(blk, LANES - ROT_DIM // 2, 1) * sin_a
                        + pltpu.roll(blk, ROT_DIM // 2, 1) * sin_b)
        return jnp.concatenate(cols, axis=1)

    o_qa, o_ka, o_va = 0, QK_W_A, 2 * QK_W_A
    o_qb = o_va + VAL_W_A
    o_kb, o_vb = o_qb + KEY_W_B, o_qb + 2 * KEY_W_B
    o_gb = o_vb + VAL_W_B
    qa = qknorm_rope(seg(o_qa, o_qa + QK_W_A), gq_ref)
    qa = qa * (HEAD_DIM_A ** -0.5 * LOG2E)
    ka = qknorm_rope(seg(o_ka, o_ka + QK_W_A), gk_ref)
    va = seg(o_va, o_va + VAL_W_A)
    qb = seg(o_qb, o_qb + KEY_W_B) * (KEY_DIM_B ** -0.5)
    kb = seg(o_kb, o_kb + KEY_W_B)
    vb = seg(o_vb, o_vb + VAL_W_B)
    gb = seg(o_gb, o_gb + VAL_W_B)
    sg = gb * (1.0 / (1.0 + jnp.exp(-gb)))
    a_low = _dot(hn, wlow_ref[...])
    gate = _dot(a_low.astype(BF16), wa2_ref[...]) + ba_ref[...]
    log_a = (jnp.minimum(gate, 0.0) - jnp.log(1.0 + jnp.exp(-jnp.abs(gate)))) * (1.0 / GATE_TAU)

    if transposed:
        qT_ref, k_ref, vT_ref, kf_ref, vf_ref, qb_ref, kb_ref, vb_ref, sg_ref, la_ref = out_refs
        for r in range(tm // tq):
            qT_ref[0, r] = qa[r * tq:(r + 1) * tq, :].T.astype(BF16)
        for r in range(tm // tk):
            vT_ref[0, r] = va[r * tk:(r + 1) * tk, :].T.astype(BF16)
        k_ref[0] = ka.astype(BF16)
    else:
        q_ref, kf_ref, vf_ref, qb_ref, kb_ref, vb_ref, sg_ref, la_ref = out_refs
        q_ref[0] = qa.astype(BF16)
    for h in range(N_HEADS_A):
        hs = slice(h * HEAD_V_A, (h + 1) * HEAD_V_A)
        kf_ref[0, pl.ds(h, tm, stride=N_HEADS_A), :] = ka[:, hs]
        vf_ref[0, pl.ds(h, tm, stride=N_HEADS_A), :] = va[:, hs]
    qb_ref[0] = qb.astype(BF16)
    kb_ref[0] = kb.astype(BF16)
    vb_ref[0] = vb.astype(BF16)
    sg_ref[0] = sg.astype(BF16)
    la_ref[0] = log_a


def _proj(x, rope, w, *, tm, transposed, tq=256, tk=256, lead_rows=0):
    nb, s, d = x.shape
    assert s % tm == 0
    grid = (s // tm, nb)

    def tok(width, dtype):
        return (jax.ShapeDtypeStruct((nb, s, width), dtype),
                pl.BlockSpec((1, tm, width), lambda i, b: (b, i, 0)))

    def const(a):
        return pl.BlockSpec(a.shape, lambda i, b: (0,) * a.ndim)

    outs = []
    if transposed:
        assert tm % tq == 0 and tm % tk == 0
        outs.append((jax.ShapeDtypeStruct((nb, s // tq, QK_W_A, tq), BF16),
                     pl.BlockSpec((1, tm // tq, QK_W_A, tq), lambda i, b: (b, i, 0, 0))))
        outs.append(tok(QK_W_A, BF16))
        outs.append((jax.ShapeDtypeStruct((nb, s // tk, VAL_W_A, tk), BF16),
                     pl.BlockSpec((1, tm // tk, VAL_W_A, tk), lambda i, b: (b, i, 0, 0))))
    else:
        outs.append(tok(QK_W_A, BF16))
    kv_rows = tm * N_HEADS_A
    kv_shape = jax.ShapeDtypeStruct((nb, (lead_rows + s) * N_HEADS_A, HEAD_V_A), F32)
    if lead_rows:
        kv_spec = pl.BlockSpec((pl.Element(1), pl.Element(kv_rows), pl.Element(HEAD_V_A)),
                               lambda i, b: (b, pl.multiple_of(
                                   lead_rows * N_HEADS_A + i * kv_rows, 8), 0))
    else:
        kv_spec = pl.BlockSpec((1, kv_rows, HEAD_V_A), lambda i, b: (b, i, 0))
    outs += [(kv_shape, kv_spec), (kv_shape, kv_spec), tok(KEY_W_B, BF16), tok(KEY_W_B, BF16),
             tok(VAL_W_B, BF16), tok(VAL_W_B, BF16), tok(KEY_W_B, F32)]
    consts = [w['gmix'], w['w_main'], w['w_low'], w['w_a2'], w['b_a'], w['gq'], w['gk'], w['gsum']]
    return pl.pallas_call(
        functools.partial(_proj_kernel, transposed=transposed, tq=tq, tk=tk),
        grid=grid,
        in_specs=[pl.BlockSpec((1, tm, d), lambda i, b: (b, i, 0)),
                  pl.BlockSpec((tm, 3 * LANES), lambda i, b: (i, 0))] + [const(a) for a in consts],
        out_specs=[o[1] for o in outs],
        out_shape=[o[0] for o in outs],
        compiler_params=_cparams(2),
        name="proj_T" if transposed else "proj_small",
    )(x, rope, *consts)


def _colmax(s):
    rows = s.shape[0]
    slabs = 8 if rows % 64 == 0 else 1
    if slabs > 1:
        s = jnp.max(s.reshape(slabs, rows // slabs, s.shape[1]), axis=0)
    return jnp.max(s, axis=0, keepdims=True)


def _attn_kernel(lam_ref, qT_ref, k_ref, vT_ref, km_ref, vmT_ref, g_ref, o_ref,
                 qz_sc, s_sc, smax_sc, m_sc, acc_sc, *, tq, tk, cw, n_meta, hps):
    i = pl.program_id(2)
    n_cb = 2 * tq // cw
    chains = [(hh, cb) for hh in range(hps) for cb in range(n_cb)]

    def hrows(hh):
        return slice(hh * HEAD_V_A, (hh + 1) * HEAD_V_A)

    for hh in range(hps):
        qT = qT_ref[0, 0, hrows(hh), :]
        row = lax.broadcasted_iota(jnp.int32, qT.shape, 0)
        zero = jnp.zeros_like(qT)
        qz_sc[hh, :, :tq] = jnp.where(row < HEAD_DIM_A, qT, zero)
        qz_sc[hh, :, tq:] = jnp.where(row >= HEAD_DIM_A, qT, zero)

    def ext(vt):
        return jnp.concatenate([vt, jnp.ones((BF16_ROWS, vt.shape[1]), BF16)], axis=0)

    def scores(j, hh, cb, rows=tk):
        kt = k_ref[0, pl.ds(pl.multiple_of(j * tk, tk), rows), hrows(hh)]
        return _dot(kt, qz_sc[hh, :, cb * cw:(cb + 1) * cw])

    def update(hh, cb, s, smax, vt):
        cs = slice(cb * cw, (cb + 1) * cw)
        m_old = m_sc[hh, :, cs]
        m_new = jnp.maximum(m_old, smax)
        alpha = jnp.exp2(m_old - m_new)
        p = jnp.exp2(s - m_new).astype(BF16)
        m_sc[hh, :, cs] = m_new
        acc_sc[hh, :, cs] = alpha * acc_sc[hh, :, cs] + _dot(vt, p)

    look = ATTN_LOOKAHEAD
    n_ch = len(chains)
    assert n_ch % look == 0

    def put(slot, s):
        s_sc[slot] = s
        smax_sc[slot] = _colmax(s)

    def prefetch(j, n):
        nn = n + look
        hh, cb = chains[nn % n_ch]
        put(n % look, scores(j + nn // n_ch, hh, cb))

    s_meta = [_dot(km_ref[:, hrows(hh)], qz_sc[hh, :, cb * cw:(cb + 1) * cw]) for hh, cb in chains]
    for n in range(look):
        put(n, scores(0, *chains[n]))
    for n, (hh, cb) in enumerate(chains):
        cs = slice(cb * cw, (cb + 1) * cw)
        s = s_meta[n]
        m0 = jnp.max(s, axis=0, keepdims=True)
        p = jnp.concatenate([jnp.exp2(s - m0).astype(BF16),
                             jnp.zeros((LANES - n_meta, cw), BF16)], axis=0)
        m_sc[hh, :, cs] = m0
        acc_sc[hh, :, cs] = _dot(ext(vmT_ref[hh]), p)

    def full_tile(j):
        for n, (hh, cb) in enumerate(chains):
            s = s_sc[n % look]
            smax = smax_sc[n % look]
            prefetch(j, n)
            update(hh, cb, s, smax, ext(vT_ref[0, j, hrows(hh), :]))

    def body(jj, carry):
        for u in range(ATTN_TILES_PER_TRIP):
            full_tile(ATTN_TILES_PER_TRIP * jj + u)
        return carry

    shift = ATTN_TILES_PER_TRIP.bit_length() - 1
    lax.fori_loop(0, lax.shift_right_logical(i, shift), body, 0)
    done = lax.shift_left(lax.shift_right_logical(i, shift), shift)
    bit = ATTN_TILES_PER_TRIP // 2
    while bit:
        @pl.when((i & bit) != 0)
        def _(done=done, bit=bit):
            for u in range(bit):
                full_tile(done + u)
        done = done + (i & bit)
        bit //= 2

    def diag_rows(cb):
        return min(tk, (((cb * cw) % tq + cw - 1) // CHUNK + 1) * CHUNK)

    for n, (hh, cb) in enumerate(chains):
        rows = diag_rows(cb)
        s = s_sc[n % look, 0:rows, :]
        if n + look < n_ch:
            hh2, cb2 = chains[n + look]
            s_sc[n % look, 0:diag_rows(cb2), :] = scores(i, hh2, cb2, diag_rows(cb2))
        kr = lax.broadcasted_iota(jnp.int32, s.shape, 0) // CHUNK
        qc = (lax.broadcasted_iota(jnp.int32, s.shape, 1) + (cb * cw) % tq) // CHUNK
        s = jnp.where(kr <= qc, s, NEG)
        update(hh, cb, s, _colmax(s), ext(vT_ref[0, i, hrows(hh), 0:rows]))

    lam = _diff_lambda(lam_ref)
    for hh in range(hps):
        o = acc_sc[hh, 0:HEAD_V_A, :] * (1.0 / acc_sc[hh, HEAD_V_A:HEAD_V_A + 1, :])
        oT = o[:, :tq] - lam * o[:, tq:]
        ms = jnp.mean(oT * oT, axis=0, keepdims=True)
        on = (oT * lax.rsqrt(ms + EPS)).T
        o_ref[0, :, hrows(hh)] = (on * g_ref[:, hrows(hh)] * (1.0 - LAMBDA_INIT)).astype(BF16)


def _attn(lam4, qT, k, vT, km, vmT, g_diff, *, n_meta):
    nb, nq, _, tq = qT.shape
    _, nk, _, tk = vT.shape
    s = k.shape[1]
    assert tq == tk and n_meta % 16 == 0
    cw = min(ATTN_COL_BLOCK, tq)
    hps = ATTN_HEADS_PER_STEP
    hw = hps * HEAD_V_A
    grid = (nb, N_HEADS_A // hps, nq)
    return pl.pallas_call(
        functools.partial(_attn_kernel, tq=tq, tk=tk, cw=cw, n_meta=n_meta, hps=hps),
        grid=grid,
        in_specs=[pl.BlockSpec(lam4.shape, lambda b, g, i: (0, 0)),
                  pl.BlockSpec((1, 1, hw, tq), lambda b, g, i: (b, i, g, 0)),
                  pl.BlockSpec((1, s, hw), lambda b, g, i: (b, 0, g)),
                  pl.BlockSpec((1, nk, hw, tk), lambda b, g, i: (b, 0, g, 0)),
                  pl.BlockSpec((n_meta, hw), lambda b, g, i: (0, g)),
                  pl.BlockSpec((hps, HEAD_V_A, LANES), lambda b, g, i: (g, 0, 0)),
                  pl.BlockSpec((1, hw), lambda b, g, i: (0, g))],
        out_specs=pl.BlockSpec((1, tq, hw), lambda b, g, i: (b, i, g)),
        out_shape=jax.ShapeDtypeStruct((nb, s, VAL_W_A), BF16),
        scratch_shapes=[pltpu.VMEM((hps, HEAD_V_A, 2 * tq), BF16),
                        pltpu.VMEM((ATTN_LOOKAHEAD, tk, cw), F32),
                        pltpu.VMEM((ATTN_LOOKAHEAD, 1, cw), F32),
                        pltpu.VMEM((hps, 1, 2 * tq), F32),
                        pltpu.VMEM((hps, HEAD_V_A + BF16_ROWS, 2 * tq), F32)],
        compiler_params=_cparams(3),
        name="attn",
    )(lam4, qT, k, vT, km, vmT, g_diff)


def _sattn_kernel(lam_ref, q_ref, ck_ref, cv_ref, kn_ref, vn_ref, g_ref, o_ref, *, p_main, p_all):
    lam = _diff_lambda(lam_ref)
    lane = lax.broadcasted_iota(jnp.int32, (1, LANES), 1)
    for h in range(N_HEADS_A):
        hs = slice(h * HEAD_V_A, (h + 1) * HEAD_V_A)
        qh = q_ref[0, :, hs]
        t = qh.shape[0]

        def head_rows(ref, lo, n):
            return ref[0, pl.ds(lo * N_HEADS_A + h, n, stride=N_HEADS_A), :]

        ks = [head_rows(ck_ref, 0, p_main)]
        vs = [head_rows(cv_ref, 0, p_main)]
        if p_all > p_main:
            ks.append(jnp.concatenate([head_rows(ck_ref, p_main, p_all - p_main),
                                       head_rows(kn_ref, 0, t)], axis=0))
            vs.append(jnp.concatenate([head_rows(cv_ref, p_main, p_all - p_main),
                                       head_rows(vn_ref, 0, t)], axis=0))
        else:
            ks.append(head_rows(kn_ref, 0, t))
            vs.append(head_rows(vn_ref, 0, t))
        ks = [a.astype(BF16) for a in ks]
        vs = [a.astype(BF16) for a in vs]
        zero = jnp.zeros_like(qh)
        q2 = jnp.concatenate([jnp.where(lane < HEAD_DIM_A, qh, zero),
                              jnp.where(lane < HEAD_DIM_A, zero, qh)], axis=0)
        ss = [_dot_nt(q2, kk) for kk in ks]
        m = functools.reduce(jnp.maximum, [jnp.max(a, axis=-1, keepdims=True) for a in ss])
        ps = [jnp.exp2(a - m) for a in ss]
        l = sum(jnp.sum(a, axis=-1, keepdims=True) for a in ps)
        acc = sum(_dot(a.astype(BF16), vv) for a, vv in zip(ps, vs))
        o2 = acc * (1.0 / l)
        o = o2[:t] - lam * o2[t:]
        ms = jnp.mean(o * o, axis=-1, keepdims=True)
        y = o * lax.rsqrt(ms + EPS) * g_ref[:, hs] * (1.0 - LAMBDA_INIT)
        o_ref[0, :, hs] = y.astype(BF16)


def _sattn(lam4, q, ck, cv, kn, vn, g_diff):
    db, t, _ = q.shape
    p_all = ck.shape[1] // N_HEADS_A
    p_main = (p_all // LANES) * LANES
    assert (p_all - p_main + t) % 16 == 0

    def tokspec(n):
        return pl.BlockSpec((1, n, VAL_W_A), lambda b: (b, 0, 0))

    def rowspec(n):
        return pl.BlockSpec((1, n * N_HEADS_A, HEAD_V_A), lambda b: (b, 0, 0))

    return pl.pallas_call(
        functools.partial(_sattn_kernel, p_main=p_main, p_all=p_all),
        grid=(db,),
        in_specs=[pl.BlockSpec(lam4.shape, lambda b: (0, 0)), tokspec(t), rowspec(p_all),
                  rowspec(p_all), rowspec(t), rowspec(t),
                  pl.BlockSpec((1, VAL_W_A), lambda b: (0, 0))],
        out_specs=tokspec(t),
        out_shape=jax.ShapeDtypeStruct((db, t, VAL_W_A), BF16),
        compiler_params=_cparams(1),
        name="sattn",
    )(lam4, q, ck, cv, kn, vn, g_diff)


def _gla_kernel(q_ref, k_ref, v_ref, la_ref, sg_ref, s0_ref, g_ref, lbd_ref, y_ref, sout_ref,
                st_sc, o_sc, *, n_chunks):
    t = pl.program_id(1)
    n_pairs = N_HEADS_B // 2

    @pl.when(t == 0)
    def _():
        for p in range(n_pairs):
            st_sc[p] = s0_ref[0, p].T

    la = la_ref[0]
    hi, lo = _split_bf16(la)
    lbd = lbd_ref[...]
    wb = lbd.shape[0]
    b = jnp.concatenate([_dot(lbd, hi[r * wb:(r + 1) * wb]) + _dot(lbd, lo[r * wb:(r + 1) * wb])
                         for r in range(la.shape[0] // wb)], axis=0)
    q = q_ref[0].astype(F32)
    k = k_ref[0].astype(F32)
    qf = q * jnp.exp(b)
    kdec = (k * jnp.exp(-b)).astype(BF16)
    first = lax.broadcasted_iota(jnp.int32, (1, LANES), 1) < KEY_DIM_B
    tr = lax.broadcasted_iota(jnp.int32, (2 * CHUNK, CHUNK), 0) & (CHUNK - 1)
    tc = lax.broadcasted_iota(jnp.int32, (2 * CHUNK, CHUNK), 1)
    causal = tc <= tr
    blocks = [(c, p) for c in range(n_chunks) for p in range(n_pairs)]

    def rows_of(c):
        return slice(c * CHUNK, (c + 1) * CHUNK)

    def lanes_of(p):
        return slice(p * LANES, (p + 1) * LANES)

    def stack_heads(x):
        return jnp.concatenate([jnp.where(first, x, 0.0), jnp.where(first, 0.0, x)], axis=0)

    qm, sc = {}, {}
    for c, p in blocks:
        qm[c, p] = stack_heads(qf[rows_of(c), lanes_of(p)]).astype(BF16)
        sc[c, p] = _dot_nt(qm[c, p], kdec[rows_of(c), lanes_of(p)])
    ds, decay = {}, {}
    for c in range(n_chunks):
        b_last = b[(c + 1) * CHUNK - 1:(c + 1) * CHUNK, :]
        decay[c] = jnp.exp(b_last)
        kd2 = k[rows_of(c)] * jnp.exp(b_last - b[rows_of(c)])
        for p in range(n_pairs):
            v2 = jnp.concatenate([v_ref[0, rows_of(c), (2 * p + a) * VAL_DIM_B:(2 * p + a + 1) * VAL_DIM_B]
                                  for a in range(2)], axis=0)
            ds[c, p] = _dot_tn(v2, stack_heads(kd2[:, lanes_of(p)]).astype(BF16))
    st_in = {}
    for p in range(n_pairs):
        st = st_sc[p]
        for c in range(n_chunks):
            st_in[c, p] = st.astype(BF16)
            st = st * decay[c][:, lanes_of(p)] + ds[c, p]
        st_sc[p] = st
    for c, p in blocks:
        o_inter = _dot_nt(qm[c, p], st_in[c, p])
        s_cp = jnp.where(causal, sc[c, p], 0.0).astype(BF16)
        o_intra = _dot(s_cp, v_ref[0, rows_of(c), 2 * p * VAL_DIM_B:(2 * p + 2) * VAL_DIM_B])
        for a in range(2):
            h = 2 * p + a
            o_sc[rows_of(c), h * VAL_DIM_B:(h + 1) * VAL_DIM_B] = (
                o_inter[a * CHUNK:(a + 1) * CHUNK]
                + o_intra[a * CHUNK:(a + 1) * CHUNK, a * VAL_DIM_B:(a + 1) * VAL_DIM_B])

    for h in range(N_HEADS_B):
        hs = slice(h * VAL_DIM_B, (h + 1) * VAL_DIM_B)
        o = o_sc[:, hs]
        ms = jnp.mean(o * o, axis=-1, keepdims=True)
        y = o * lax.rsqrt(ms + EPS) * g_ref[:, hs]
        y_ref[0, :, hs] = (y * sg_ref[0, :, hs].astype(F32)).astype(BF16)

    @pl.when(t == pl.num_programs(1) - 1)
    def _():
        for p in range(n_pairs):
            sout_ref[0, p] = st_sc[p].T


def _gla(q, k, v, la, sg, s0, g_gla, *, tg):
    nb, s, _ = q.shape
    assert s % tg == 0 and tg % CHUNK == 0
    n_chunks = tg // CHUNK
    s0p = s0.reshape(s0.shape[0], N_HEADS_B // 2, 2 * KEY_DIM_B, VAL_DIM_B)
    shared = s0p.shape[0] == 1
    wb = min(tg, MXU_WIDTH)
    idx = jnp.arange(wb)
    lbd = ((idx[:, None] // CHUNK == idx[None, :] // CHUNK)
           & (idx[None, :] <= idx[:, None])).astype(BF16)

    def tok(width):
        return pl.BlockSpec((1, tg, width), lambda b, t: (b, t, 0))

    sspec_in = pl.BlockSpec((1,) + s0p.shape[1:], (lambda b, t: (0, 0, 0, 0)) if shared
                            else (lambda b, t: (b, 0, 0, 0)))
    y, s_out = pl.pallas_call(
        functools.partial(_gla_kernel, n_chunks=n_chunks),
        grid=(nb, s // tg),
        in_specs=[tok(KEY_W_B), tok(KEY_W_B), tok(VAL_W_B), tok(KEY_W_B), tok(VAL_W_B), sspec_in,
                  pl.BlockSpec((1, VAL_W_B), lambda b, t: (0, 0)),
                  pl.BlockSpec((wb, wb), lambda b, t: (0, 0))],
        out_specs=[tok(VAL_W_B), pl.BlockSpec((1,) + s0p.shape[1:], lambda b, t: (b, 0, 0, 0))],
        out_shape=[jax.ShapeDtypeStruct((nb, s, VAL_W_B), BF16),
                   jax.ShapeDtypeStruct((nb,) + s0p.shape[1:], F32)],
        scratch_shapes=[pltpu.VMEM((N_HEADS_B // 2, VAL_DIM_B, 2 * KEY_DIM_B), F32),
                        pltpu.VMEM((tg, VAL_W_B), F32)],
        compiler_params=_cparams(2),
        name="gla",
    )(q, k, v, la, sg, s0p, g_gla, lbd)
    return y, s_out.reshape(nb, N_HEADS_B, KEY_DIM_B, VAL_DIM_B)


def _ffn_kernel(x_ref, ya_ref, yb_ref, woa_ref, wob_ref, gn_ref, wg_ref, wu_ref, wd_ref, o_ref,
                *, n_split):
    h = x_ref[...] + _dot(ya_ref[...], woa_ref[...]) + _dot(yb_ref[...], wob_ref[...])
    ms = jnp.mean(h * h, axis=-1, keepdims=True)
    hn = (h * lax.rsqrt(ms + EPS) * gn_ref[...]).astype(BF16)
    tiles = D_FF // MXU_WIDTH
    bounds = [MXU_WIDTH * ((tiles * c + n_split - 1) // n_split) for c in range(n_split + 1)]
    f = None
    for c in range(n_split):
        cs = slice(bounds[c], bounds[c + 1])
        g = _dot(hn, wg_ref[:, cs])
        u = _dot(hn, wu_ref[:, cs])
        a = (g * (1.0 / (1.0 + jnp.exp(-g))) * u).astype(BF16)
        d = _dot(a, wd_ref[cs, :])
        f = d if f is None else f + d
    o_ref[...] = h + f


def _ffn(x, ya, yb, w, *, tm, n_split=2):
    t, d = x.shape
    assert t % tm == 0

    def const(a):
        return pl.BlockSpec(a.shape, lambda i: (0,) * a.ndim, pipeline_mode=pl.Buffered(1))

    consts = [w['w_out_a'], w['w_out_b'], w['g_ffn'], w['w_gate'], w['w_up'], w['w_down']]
    return pl.pallas_call(
        functools.partial(_ffn_kernel, n_split=n_split),
        grid=(t // tm,),
        in_specs=[pl.BlockSpec((tm, d), lambda i: (i, 0)),
                  pl.BlockSpec((tm, VAL_W_A), lambda i: (i, 0)),
                  pl.BlockSpec((tm, VAL_W_B), lambda i: (i, 0))] + [const(a) for a in consts],
        out_specs=pl.BlockSpec((tm, d), lambda i: (i, 0)),
        out_shape=jax.ShapeDtypeStruct((t, d), F32),
        compiler_params=_cparams(1),
        name="ffn",
    )(x, ya, yb, *consts)


def _put_lead_kernel(k_any, v_any, km_ref, vm_ref, ko_ref, vo_ref):
    del k_any, v_any
    ko_ref[0] = km_ref[0]
    vo_ref[0] = vm_ref[0]


def _put_lead_rows(k, v, k_lead, v_lead):
    nb = k.shape[0]
    n_lead = k_lead.shape[1]
    lead = pl.BlockSpec((1, n_lead, HEAD_V_A), lambda b: (0, 0, 0))
    out = pl.BlockSpec((1, n_lead, HEAD_V_A), lambda b: (b, 0, 0))
    return pl.pallas_call(
        _put_lead_kernel,
        grid=(nb,),
        in_specs=[pl.BlockSpec(memory_space=pl.ANY), pl.BlockSpec(memory_space=pl.ANY), lead, lead],
        out_specs=[out, out],
        out_shape=[jax.ShapeDtypeStruct(k.shape, k.dtype), jax.ShapeDtypeStruct(v.shape, v.dtype)],
        input_output_aliases={0: 0, 1: 1},
        compiler_params=_cparams(1),
        name="put_lead",
    )(k, v, k_lead, v_lead)


def _rope_table(pos):
    half = ROT_DIM // 2
    inv_freq = ROPE_THETA ** (-jnp.arange(0, ROT_DIM, 2, dtype=F32) / ROT_DIM)
    ang = pos.astype(F32)[:, None] * inv_freq[None, :]
    cos, sin = jnp.cos(ang), jnp.sin(ang)
    lane = jnp.arange(LANES) % HEAD_DIM_A
    j = lane % half
    c = jnp.where(lane[None, :] < ROT_DIM, cos[:, j], 1.0)
    sa = jnp.where(lane[None, :] < half, -sin[:, j], 0.0)
    sb = jnp.where((lane[None, :] >= half) & (lane[None, :] < ROT_DIM), sin[:, j], 0.0)
    return jnp.concatenate([c, sa, sb], axis=1).astype(F32)


def kernel(x_prompt, x_sample, cache_k_diff, cache_v_diff, state_gla, meta_tokens, norm_mix, w_in,
           w_a2, b_a, q_norm, k_norm, lambda_q1, lambda_k1, lambda_q2, lambda_k2, g_diff, g_gla,
           w_out, norm_ffn, w_ffn_gate, w_ffn_up, w_ffn_down):
    depth = w_in.shape[0]
    assert depth == 1, "single-layer step"
    bsz, seq, d = x_prompt.shape
    db, dt, _ = x_sample.shape
    n_meta = meta_tokens.shape[0]
    past = cache_k_diff.shape[2]
    assert d == D_MODEL and n_meta <= LANES and seq % CHUNK == 0

    w_in0 = w_in[0]
    gid = jnp.arange(MXU_WIDTH) // HEAD_DIM_A
    w = {
        'gmix': norm_mix[0][None, :],
        'w_main': w_in0[:, :N_MAIN].astype(BF16),
        'w_low': jnp.pad(w_in0[:, N_MAIN:], ((0, 0), (0, LANES - GATE_RANK))).astype(BF16),
        'w_a2': jnp.pad(w_a2[0], ((0, LANES - GATE_RANK), (0, 0))).astype(BF16),
        'b_a': b_a[0][None, :],
        'gq': jnp.tile(q_norm[0], QK_W_A // HEAD_DIM_A)[None, :],
        'gk': jnp.tile(k_norm[0], QK_W_A // HEAD_DIM_A)[None, :],
        'gsum': (gid[:, None] == gid[None, :]).astype(BF16),
        'w_out_a': w_out[0][:VAL_W_A].astype(BF16),
        'w_out_b': w_out[0][VAL_W_A:].astype(BF16),
        'g_ffn': norm_ffn[0][None, :],
        'w_gate': w_ffn_gate[0].astype(BF16),
        'w_up': w_ffn_up[0].astype(BF16),
        'w_down': w_ffn_down[0].astype(BF16),
    }
    lam4 = jnp.stack([lambda_q1[0], lambda_k1[0], lambda_q2[0], lambda_k2[0]])
    g_diff2 = g_diff[0][None, :]
    g_gla2 = g_gla[0][None, :]

    rope_m = _rope_table(jnp.arange(n_meta, dtype=jnp.int32))
    (_, kf_m, vf_m, qb_m, kb_m, vb_m, sg_m, la_m) = _proj(
        meta_tokens[None], rope_m, w, tm=n_meta, transposed=False)
    pad_m = ((0, 0), (0, CHUNK - n_meta), (0, 0))
    s_zero = jnp.zeros((1, N_HEADS_B, KEY_DIM_B, VAL_DIM_B), F32)
    _, s_meta = _gla(jnp.pad(qb_m, pad_m), jnp.pad(kb_m, pad_m), jnp.pad(vb_m, pad_m),
                     jnp.pad(la_m, pad_m), jnp.pad(sg_m, pad_m), s_zero, g_gla2, tg=CHUNK)

    tm = 512 if seq % 512 == 0 else seq
    tqk = 512 if seq % 512 == 0 else seq
    tg = 2 * tm if seq % (2 * tm) == 0 else tm
    rope_x = _rope_table(n_meta + jnp.arange(seq, dtype=jnp.int32))
    (qT, k_bf, vT, kf_x, vf_x, qb_x, kb_x, vb_x, sg_x, la_x) = _proj(
        x_prompt, rope_x, w, tm=tg, transposed=True, tq=tqk, tk=tqk, lead_rows=n_meta)
    km = kf_m.reshape(n_meta, QK_W_A).astype(BF16)
    vmT = jnp.pad(vf_m.reshape(n_meta, N_HEADS_A, HEAD_V_A).transpose(1, 2, 0),
                  ((0, 0), (0, 0), (0, LANES - n_meta))).astype(BF16)
    ya_x = _attn(lam4, qT, k_bf, vT, km, vmT, g_diff2, n_meta=n_meta)
    tgla = 2 * tg if seq % (2 * tg) == 0 else tg
    yb_x, s_final = _gla(qb_x, kb_x, vb_x, la_x, sg_x, s_meta, g_gla2, tg=tgla)
    y_prompt = _ffn(x_prompt.reshape(bsz * seq, d), ya_x.reshape(bsz * seq, VAL_W_A),
                    yb_x.reshape(bsz * seq, VAL_W_B), w, tm=tg).reshape(bsz, seq, d)
    new_k_p, new_v_p = _put_lead_rows(kf_x, vf_x, kf_m, vf_m)

    ts = db * dt
    rope_s = jnp.tile(_rope_table(past + jnp.arange(dt, dtype=jnp.int32)), (db, 1))
    (q_s, kf_s, vf_s, qb_s, kb_s, vb_s, sg_s, la_s) = _proj(
        x_sample.reshape(1, ts, d), rope_s, w, tm=ts, transposed=False)
    ya_s = _sattn(lam4, q_s.reshape(db, dt, QK_W_A),
                  cache_k_diff.reshape(db, past * N_HEADS_A, HEAD_V_A),
                  cache_v_diff.reshape(db, past * N_HEADS_A, HEAD_V_A),
                  kf_s.reshape(db, dt * N_HEADS_A, HEAD_V_A),
                  vf_s.reshape(db, dt * N_HEADS_A, HEAD_V_A), g_diff2)
    pad_s = ((0, 0), (0, CHUNK - dt), (0, 0))

    def stream(a):
        return jnp.pad(a.reshape(db, dt, a.shape[-1]), pad_s)

    yb_s, s_new = _gla(stream(qb_s), stream(kb_s), stream(vb_s), stream(la_s), stream(sg_s),
                       state_gla[0], g_gla2, tg=CHUNK)
    y_sample = _ffn(x_sample.reshape(ts, d), ya_s.reshape(ts, VAL_W_A),
                    yb_s[:, :dt].reshape(ts, VAL_W_B), w, tm=ts).reshape(db, dt, d)

    return (y_prompt, y_sample,
            new_k_p.reshape(1, bsz, n_meta + seq, N_HEADS_A, 2 * HEAD_DIM_A),
            new_v_p.reshape(1, bsz, n_meta + seq, N_HEADS_A, HEAD_V_A),
            s_final[None],
            kf_s.reshape(1, db, dt, N_HEADS_A, 2 * HEAD_DIM_A),
            vf_s.reshape(1, db, dt, N_HEADS_A, HEAD_V_A),
            s_new[None])
```
